```python
import math
import jax, jax.numpy as jnp
from jax import lax
import numpy as np


D_MODEL = 1024
BATCH = 4
SEQ = 8192
DEPTH = 1

CHUNK = 64
EPS = 1e-6
A_HEADS = 8
A_DK = 128
A_DV = 256
CONV_W = 4
B_HEADS = 4
B_DK = 256
B_DV = 512
ROPE_BASE = 10000.0
D_FF = 4 * D_MODEL

A_QK = A_HEADS * A_DK
A_V = A_HEADS * A_DV
B_QK = B_HEADS * B_DK
B_V = B_HEADS * B_DV
IN_SIZES = (A_QK, A_QK, A_V, A_V, A_HEADS, A_HEADS, B_QK, B_QK, B_V, B_V, D_MODEL, D_MODEL)
D_IN = sum(IN_SIZES)

kernel_name = "hybrid_gdn_retention_sandwich_block"


def rmsnorm(x, w):
    xf = x.astype(jnp.float32)
    y = xf * lax.rsqrt(jnp.mean(xf * xf, axis=-1, keepdims=True) + EPS)
    return (y * w.astype(jnp.float32)).astype(x.dtype)


def l2norm(x):
    xf = x.astype(jnp.float32)
    return (xf * lax.rsqrt(jnp.sum(xf * xf, axis=-1, keepdims=True) + EPS)).astype(x.dtype)


def _split_cols(t, sizes):
    out, start = [], 0
    for s in sizes:
        out.append(t[..., start:start + s])
        start += s
    return out


def causal_depthwise_conv(x, w):
    k_taps, s = w.shape[0], x.shape[1]
    xp = jnp.pad(x, ((0, 0), (k_taps - 1, 0), (0, 0)))
    y = xp[:, 0:s] * w[0]
    for j in range(1, k_taps):
        y = y + xp[:, j:j + s] * w[j]
    return y


def rotary(t, positions):
    d = t.shape[-1]
    inv_freq = ROPE_BASE ** (-jnp.arange(0, d, 2, dtype=jnp.float32) / d)
    ang = positions.astype(jnp.float32)[:, None] * inv_freq[None, :]
    cos = jnp.cos(ang)[None, :, None, :]
    sin = jnp.sin(ang)[None, :, None, :]
    t1, t2 = t[..., :d // 2], t[..., d // 2:]
    return jnp.concatenate([t1 * cos - t2 * sin, t1 * sin + t2 * cos], axis=-1).astype(t.dtype)


def to_chunks(t):
    b, s, h = t.shape[:3]
    t = t.reshape(b, s // CHUNK, CHUNK, h, *t.shape[3:])
    return jnp.moveaxis(t, 3, 1)


def from_chunks(t):
    t = jnp.moveaxis(t, 1, 3)
    return t.reshape(t.shape[0], t.shape[1] * t.shape[2], *t.shape[3:])


def chunk_gated_delta_rule(q, k, v, beta, g):
    dtype = v.dtype
    q, k, v = (to_chunks(t.astype(jnp.float32)) for t in (q, k, v))
    beta, g = (to_chunks(t.astype(jnp.float32)) for t in (beta, g))
    g_cum = jnp.cumsum(g, axis=-1)
    idx = jnp.arange(CHUNK)
    causal = idx[:, None] >= idx[None, :]
    strict = idx[:, None] > idx[None, :]
    decay = jnp.exp(jnp.where(causal, g_cum[..., :, None] - g_cum[..., None, :], -jnp.inf))
    k_beta = k * beta[..., None]
    lower = jnp.where(strict, jnp.einsum('bhncd,bhnmd->bhncm', k_beta, k) * decay, 0.0)
    eye = jnp.broadcast_to(jnp.eye(CHUNK, dtype=jnp.float32), lower.shape)
    t_inv = lax.linalg.triangular_solve(lower, eye, left_side=True, lower=True, unit_diagonal=True)
    u = jnp.einsum('bhncm,bhnme->bhnce', t_inv, v * beta[..., None])
    w = jnp.einsum('bhncm,bhnmd->bhncd', t_inv, k_beta * jnp.exp(g_cum)[..., None])
    attn = jnp.where(causal, jnp.einsum('bhncd,bhnmd->bhncm', q, k) * decay, 0.0)
    q_exp = q * jnp.exp(g_cum)[..., None]
    g_last = g_cum[..., -1:]
    k_dec = k * jnp.exp(g_last - g_cum)[..., None]
    last_dec = jnp.exp(g_last[..., 0])

    def step(state, inp):
        qe, ww, uu, aqk, kd, ld = inp
        v_new = uu - jnp.einsum('bhck,bhkv->bhcv', ww, state)
        o = jnp.einsum('bhck,bhkv->bhcv', qe, state) + jnp.einsum('bhcm,bhmv->bhcv', aqk, v_new)
        state = state * ld[..., None, None] + jnp.einsum('bhck,bhcv->bhkv', kd, v_new)
        return state, o

    b, h = q.shape[0], q.shape[1]
    state0 = jnp.zeros((b, h, q.shape[-1], v.shape[-1]), jnp.float32)
    xs = tuple(jnp.moveaxis(t, 2, 0) for t in (q_exp, w, u, attn, k_dec, last_dec))
    _, o = lax.scan(step, state0, xs)
    return from_chunks(jnp.moveaxis(o, 0, 2)).astype(dtype)


def chunk_retention(q, k, v):
    dtype = v.dtype
    q, k, v = (to_chunks(t.astype(jnp.float32)) for t in (q, k, v))
    h = q.shape[1]
    log_gamma = jnp.log1p(-jnp.exp2(-5.0 - jnp.arange(h, dtype=jnp.float32)))
    pos = jnp.arange(CHUNK, dtype=jnp.float32)
    dist = jnp.abs(pos[:, None] - pos[None, :])
    intra = jnp.exp(log_gamma[:, None, None] * dist)
    scores = jnp.einsum('bhncd,bhnmd->bhncm', q, k) * intra[None, :, None]
    o_intra = jnp.einsum('bhncm,bhnme->bhnce', scores, v)
    q_dec = jnp.exp(log_gamma[:, None] * (pos + 1.0))[None, :, :, None]
    k_dec = jnp.exp(log_gamma[:, None] * (CHUNK - 1.0 - pos))[None, :, :, None]
    chunk_dec = jnp.exp(log_gamma * CHUNK)[None, :, None, None]

    def step(state, inp):
        qc, kc, vc = inp
        o = jnp.einsum('bhcd,bhde->bhce', qc * q_dec, state)
        state = state * chunk_dec + jnp.einsum('bhcd,bhce->bhde', kc * k_dec, vc)
        return state, o

    state0 = jnp.zeros((q.shape[0], h, q.shape[-1], v.shape[-1]), jnp.float32)
    xs = tuple(jnp.moveaxis(t, 2, 0) for t in (q, k, v))
    _, o_inter = lax.scan(step, state0, xs)
    o = o_intra + jnp.moveaxis(o_inter, 0, 2)
    return from_chunks(o).astype(dtype)


def hybrid_layer(x, n_mix_pre, n_mix_post, n_mlp_pre, n_mlp_post, w_in, conv_a, a_log, dt_bias,
                 norm_a, norm_b, w_br_a, w_br_b, w_out, w_up, w_down):
    b, s, _ = x.shape
    xn = rmsnorm(x, n_mix_pre)
    proj = xn @ w_in
    qa, ka, va, za, beta_logit, a_dt, qb, kb, vb, gb, gate_a, gate_b = _split_cols(proj, IN_SIZES)

    qkv = jax.nn.silu(causal_depthwise_conv(jnp.concatenate([qa, ka, va], axis=-1), conv_a))
    qa, ka, va = _split_cols(qkv, (A_QK, A_QK, A_V))
    qa = l2norm(qa.reshape(b, s, A_HEADS, A_DK)) * (A_DK ** -0.5)
    ka = l2norm(ka.reshape(b, s, A_HEADS, A_DK))
    va = va.reshape(b, s, A_HEADS, A_DV)
    beta = jax.nn.sigmoid(beta_logit)
    g = -jnp.exp(a_log) * jax.nn.softplus(a_dt + dt_bias)
    oa = chunk_gated_delta_rule(qa, ka, va, beta, g)
    oa = rmsnorm(oa, norm_a) * jax.nn.silu(za.reshape(b, s, A_HEADS, A_DV))
    ya = oa.reshape(b, s, A_V) @ w_br_a

    positions = jnp.arange(s)
    qb = rotary(qb.reshape(b, s, B_HEADS, B_DK), positions)
    kb = rotary(kb.reshape(b, s, B_HEADS, B_DK), positions) * (B_DK ** -0.5)
    vb = vb.reshape(b, s, B_HEADS, B_DV)
    ob = chunk_retention(qb, kb, vb)
    ob = rmsnorm(ob, norm_b.reshape(B_HEADS, B_DV)) * jax.nn.silu(gb.reshape(b, s, B_HEADS, B_DV))
    yb = ob.reshape(b, s, B_V) @ w_br_b

    mix = (jax.nn.sigmoid(gate_a) * ya + jax.nn.sigmoid(gate_b) * yb) @ w_out
    x = x + rmsnorm(mix, n_mix_post)

    hn = rmsnorm(x, n_mlp_pre)
    ff = jnp.square(jax.nn.relu(hn @ w_up)) @ w_down
    return x + rmsnorm(ff, n_mlp_post)


def setup_inputs(seed: int = 0) -> dict:
    key = jax.random.key(seed)
    ks = jax.random.split(key, 16)
    L = DEPTH

    def nrm(k, shape, fan_in):
        return jax.random.normal(k, shape, jnp.float32) * fan_in ** -0.5

    def gain(k, shape):
        return 1.0 + 0.05 * jax.random.normal(k, shape, jnp.float32)

    x = jax.random.normal(ks[0], (BATCH, SEQ, D_MODEL), jnp.float32)
    dt = jnp.exp(jax.random.uniform(ks[8], (L, A_HEADS), jnp.float32,
                                    minval=math.log(1e-3), maxval=math.log(1e-1)))
    dt_bias = dt + jnp.log(-jnp.expm1(-dt))
    a_log = jnp.log(jax.random.uniform(ks[9], (L, A_HEADS), jnp.float32, minval=1.0, maxval=16.0))
    return {
        "x": x,
        "norm_mix_pre": gain(ks[1], (L, D_MODEL)),
        "norm_mix_post": gain(ks[2], (L, D_MODEL)),
        "norm_mlp_pre": gain(ks[3], (L, D_MODEL)),
        "norm_mlp_post": gain(ks[4], (L, D_MODEL)),
        "w_in": nrm(ks[5], (L, D_MODEL, D_IN), D_MODEL),
        "conv_a": nrm(ks[6], (L, CONV_W, 2 * A_QK + A_V), CONV_W),
        "a_log": a_log,
        "dt_bias": dt_bias,
        "norm_a": gain(ks[7], (L, A_DV)),
        "norm_b": gain(ks[10], (L, B_V)),
        "w_br_a": nrm(ks[11], (L, A_V, D_MODEL), A_V),
        "w_br_b": nrm(ks[12], (L, B_V, D_MODEL), B_V),
        "w_out": nrm(ks[13], (L, D_MODEL, D_MODEL), D_MODEL),
        "w_up": nrm(ks[14], (L, D_MODEL, D_FF), D_MODEL),
        "w_down": nrm(ks[15], (L, D_FF, D_MODEL), D_FF),
    }


def reference(x, norm_mix_pre, norm_mix_post, norm_mlp_pre, norm_mlp_post, w_in, conv_a, a_log,
              dt_bias, norm_a, norm_b, w_br_a, w_br_b, w_out, w_up, w_down):
    for l in range(DEPTH):
        x = hybrid_layer(x, norm_mix_pre[l], norm_mix_post[l], norm_mlp_pre[l], norm_mlp_post[l],
                         w_in[l], conv_a[l], a_log[l], dt_bias[l], norm_a[l], norm_b[l],
                         w_br_a[l], w_br_b[l], w_out[l], w_up[l], w_down[l])
    return x
```

```python
import functools
import math

import jax
import jax.numpy as jnp
from jax import lax
from jax.experimental import pallas as pl
from jax.experimental.pallas import tpu as pltpu

F32 = jnp.float32
BF16 = jnp.bfloat16

CHUNK = 64
EPS = 1e-6
A_HEADS, A_DK, A_DV, CONV_W = 8, 128, 256, 4
B_HEADS, B_DK, B_DV = 4, 256, 512
ROPE_BASE = 10000.0
A_QK, A_V = A_HEADS * A_DK, A_HEADS * A_DV
B_QK, B_V = B_HEADS * B_DK, B_HEADS * B_DV

SB = 256
CPS = SB // CHUNK
LANES = 128
VMEM_LIMIT = 56 * 1024 * 1024

_NT = (((1,), (1,)), ((), ()))
_TN = (((0,), (0,)), ((), ()))


def _dot(a, b):
    return jnp.dot(a, b, preferred_element_type=F32)


def _silu(x):
    return x * jax.nn.sigmoid(x)


def _inproj_kernel(x_ref, g_ref, w_ref, ws_ref, o_ref, os_ref, xn_ref):
    @pl.when(pl.program_id(1) == 0)
    def _():
        x = x_ref[...]
        ms = jnp.mean(x * x, axis=-1, keepdims=True)
        xn = (x * lax.rsqrt(ms + EPS) * g_ref[...]).astype(BF16)
        xn_ref[...] = xn
        os_ref[...] = _dot(xn, ws_ref[...])

    o_ref[...] = _dot(xn_ref[...], w_ref[...]).astype(o_ref.dtype)


def _inproj(x2, gain, w_main, w_small, tm, tn):
    t, d = x2.shape
    n = w_main.shape[1]
    return pl.pallas_call(
        _inproj_kernel,
        grid=(t // tm, n // tn),
        in_specs=[
            pl.BlockSpec((tm, d), lambda i, j: (i, 0)),
            pl.BlockSpec((1, d), lambda i, j: (0, 0)),
            pl.BlockSpec((d, tn), lambda i, j: (0, j)),
            pl.BlockSpec((d, LANES), lambda i, j: (0, 0)),
        ],
        out_specs=[
            pl.BlockSpec((tm, tn), lambda i, j: (i, j)),
            pl.BlockSpec((tm, LANES), lambda i, j: (i, 0)),
        ],
        out_shape=[
            jax.ShapeDtypeStruct((t, n), BF16),
            jax.ShapeDtypeStruct((t, LANES), F32),
        ],
        scratch_shapes=[pltpu.VMEM((tm, d), BF16)],
        compiler_params=pltpu.CompilerParams(
            dimension_semantics=("arbitrary", "arbitrary"), vmem_limit_bytes=VMEM_LIMIT),
        name="inproj",
    )(x2, gain, w_main, w_small)


def _conv_silu(e_ref, x_ref, c_ref, lb):
    e_ref[pl.ds(8, lb), :] = x_ref[...].astype(F32)
    acc = e_ref[pl.ds(8 - (CONV_W - 1), lb), :] * c_ref[0:1, :]
    for j in range(1, CONV_W):
        acc = acc + e_ref[pl.ds(8 - (CONV_W - 1) + j, lb), :] * c_ref[j:j + 1, :]
    e_ref[pl.ds(0, 8), :] = e_ref[pl.ds(lb, 8), :]
    return _silu(acc)


def _l2norm(y):
    return y * lax.rsqrt(jnp.sum(y * y, axis=-1, keepdims=True) + EPS)


def _group_cumsum_rows(x, group):
    lane = lax.broadcasted_iota(jnp.int32, x.shape, 1)
    pos = lane & (group - 1)
    s = 1
    while s < group:
        shifted = pltpu.roll(x, s, axis=1)
        x = x + jnp.where(pos >= s, shifted, 0.0)
        s *= 2
    return x


def _gdn_kernel(q_ref, k_ref, v_ref, z_ref, st_ref, alog_ref, dtb_ref, cq_ref, ck_ref, cv_ref,
                na_ref, o_ref,
                eq_ref, ek_ref, ev_ref, qs_ref, ks_ref, vs_ref, bc_ref, gcc_ref, row_ref, state_ref):
    h = pl.program_id(1)
    r = pl.program_id(2)
    lb = q_ref.shape[0]
    nsb = lb // SB

    @pl.when(r == 0)
    def _():
        eq_ref[pl.ds(0, 8), :] = jnp.zeros((8, A_DK), F32)
        ek_ref[pl.ds(0, 8), :] = jnp.zeros((8, A_DK), F32)
        ev_ref[pl.ds(0, 8), :] = jnp.zeros((8, A_DV), F32)
        state_ref[...] = jnp.zeros_like(state_ref)

    qs_ref[...] = _l2norm(_conv_silu(eq_ref, q_ref, cq_ref, lb)) * (A_DK ** -0.5)
    ks_ref[...] = _l2norm(_conv_silu(ek_ref, k_ref, ck_ref, lb))
    vs_ref[...] = _conv_silu(ev_ref, v_ref, cv_ref, lb)

    bl = st_ref[pl.ds(h, 1), :]
    adt = st_ref[pl.ds(A_HEADS + h, 1), :]
    beta_row = jax.nn.sigmoid(bl)
    xg = adt + dtb_ref[pl.ds(h, 1), :]
    softplus = jnp.maximum(xg, 0.0) + jnp.log1p(jnp.exp(-jnp.abs(xg)))
    g_row = -jnp.exp(alog_ref[pl.ds(h, 1), :]) * softplus
    gc_row = _group_cumsum_rows(jnp.broadcast_to(g_row, (8, lb)), CHUNK)
    beta8 = jnp.broadcast_to(beta_row, (8, lb))
    for s in range(nsb):
        sl = slice(s * SB, (s + 1) * SB)
        row_ref[s] = gc_row[:, sl]
        gcc_ref[pl.ds(s * SB, SB), :] = jnp.transpose(
            jnp.broadcast_to(gc_row[0:1, sl], (LANES, SB)))
        bc_ref[pl.ds(s * SB, SB), :] = jnp.transpose(
            jnp.broadcast_to(beta8[0:1, sl], (LANES, SB)))

    row = lax.broadcasted_iota(jnp.int32, (SB, SB), 0)
    col = lax.broadcasted_iota(jnp.int32, (SB, SB), 1)

    def same(bits):
        return ((row >> bits) == (col >> bits)).astype(F32)

    m16, m32, m64 = same(4), same(5), same(6)
    lower = (row >= col).astype(F32)
    causal_f = m64 * lower
    strict_f = m64 * (row > col).astype(F32)
    eye = (row == col).astype(F32)
    c1_mask = (m32 - m16) * lower
    c2_mask = (m64 - m32) * lower
    na = na_ref[...]

    def sb_body(s, carry):
        r0 = pl.multiple_of(s * SB, SB)
        q = qs_ref[pl.ds(r0, SB), :]
        k = ks_ref[pl.ds(r0, SB), :]
        v = vs_ref[pl.ds(r0, SB), :]
        bc = bc_ref[pl.ds(r0, SB), :]
        gc = gcc_ref[pl.ds(r0, SB), :]
        gr = row_ref[s][0:1, :]

        gc2 = jnp.concatenate([gc, gc], axis=1)
        decay = jnp.exp(jnp.minimum(gc2 - gr, 0.0)) * causal_f
        kb = k * bc
        k16 = k.astype(BF16)
        kq = lax.dot_general(jnp.concatenate([kb, q], axis=0).astype(BF16), k16, _NT,
                             preferred_element_type=F32)
        a = kq[:SB] * decay * strict_f
        attn = kq[SB:] * decay

        n1 = -(a * m16)
        n1b = n1.astype(BF16)
        n2b = _dot(n1b, n1b).astype(BF16)
        n4b = _dot(n2b, n2b).astype(BF16)
        n8b = _dot(n4b, n4b).astype(BF16)
        p = eye + n1
        p = p + _dot(p.astype(BF16), n2b)
        p = p + _dot(p.astype(BF16), n4b)
        p = p + _dot(p.astype(BF16), n8b)
        pb = p.astype(BF16)
        t32 = p - _dot(_dot(pb, (a * c1_mask).astype(BF16)).astype(BF16), pb)
        t32b = t32.astype(BF16)
        tinv = t32 - _dot(_dot(t32b, (a * c2_mask).astype(BF16)).astype(BF16), t32b)

        eg = jnp.exp(gc)
        bc2 = jnp.concatenate([bc, bc], axis=1)
        rhs = jnp.concatenate([v * bc2, kb * eg], axis=1).astype(BF16)
        uw = _dot(tinv.astype(BF16), rhs)
        u = uw[:, :A_DV]
        w = uw[:, A_DV:]
        qe = q * eg

        state = state_ref[...]
        vns, qss = [], []
        for c in range(CPS):
            lo = c * CHUNK
            wq = jnp.concatenate([w[lo:lo + CHUNK], qe[lo:lo + CHUNK]], axis=0).astype(BF16)
            res = _dot(wq, state.astype(BF16))
            vn = u[lo:lo + CHUNK] - res[:CHUNK]
            qss.append(res[CHUNK:])
            vns.append(vn)
            glast = gc[lo + CHUNK - 1:lo + CHUNK, :]
            kd = k[lo:lo + CHUNK] * jnp.exp(glast - gc[lo:lo + CHUNK])
            upd = lax.dot_general(kd.astype(BF16), vn.astype(BF16), _TN,
                                  preferred_element_type=F32)
            ld = jnp.exp(glast)
            state = state * jnp.concatenate([ld, ld], axis=1) + upd
        state_ref[...] = state

        vn_all = jnp.concatenate(vns, axis=0)
        o = jnp.concatenate(qss, axis=0) + _dot(attn.astype(BF16), vn_all.astype(BF16))
        ms = jnp.mean(o * o, axis=-1, keepdims=True)
        z = z_ref[pl.ds(r0, SB), :].astype(F32)
        o_ref[pl.ds(r0, SB), :] = (o * lax.rsqrt(ms + EPS) * na * _silu(z)).astype(o_ref.dtype)
        return carry

    lax.fori_loop(0, nsb, sb_body, 0)


def _gdn(proj, small_t, alog_b, dtb_b, conv_a, norm_a, batch, seq, lb):
    t = proj.shape[0]
    nr = seq // lb
    qk_blocks = A_QK // A_DK
    v_off = (2 * A_QK) // A_DV
    z_off = (2 * A_QK + A_V) // A_DV

    def rows(b, h, r):
        return b * nr + r

    return pl.pallas_call(
        _gdn_kernel,
        grid=(batch, A_HEADS, nr),
        in_specs=[
            pl.BlockSpec((lb, A_DK), lambda b, h, r: (rows(b, h, r), h)),
            pl.BlockSpec((lb, A_DK), lambda b, h, r: (rows(b, h, r), qk_blocks + h)),
            pl.BlockSpec((lb, A_DV), lambda b, h, r: (rows(b, h, r), v_off + h)),
            pl.BlockSpec((lb, A_DV), lambda b, h, r: (rows(b, h, r), z_off + h)),
            pl.BlockSpec((None, 2 * A_HEADS, lb), lambda b, h, r: (b, 0, r)),
            pl.BlockSpec((A_HEADS, lb), lambda b, h, r: (0, 0)),
            pl.BlockSpec((A_HEADS, lb), lambda b, h, r: (0, 0)),
            pl.BlockSpec((CONV_W, A_DK), lambda b, h, r: (0, h)),
            pl.BlockSpec((CONV_W, A_DK), lambda b, h, r: (0, qk_blocks + h)),
            pl.BlockSpec((CONV_W, A_DV), lambda b, h, r: (0, v_off + h)),
            pl.BlockSpec((1, A_DV), lambda b, h, r: (0, 0)),
        ],
        out_specs=pl.BlockSpec((lb, A_DV), lambda b, h, r: (rows(b, h, r), h)),
        out_shape=jax.ShapeDtypeStruct((t, A_V), BF16),
        scratch_shapes=[
            pltpu.VMEM((lb + 8, A_DK), F32),
            pltpu.VMEM((lb + 8, A_DK), F32),
            pltpu.VMEM((lb + 8, A_DV), F32),
            pltpu.VMEM((lb, A_DK), F32),
            pltpu.VMEM((lb, A_DK), F32),
            pltpu.VMEM((lb, A_DV), F32),
            pltpu.VMEM((lb, LANES), F32),
            pltpu.VMEM((lb, LANES), F32),
            pltpu.VMEM((lb // SB, 8, SB), F32),
            pltpu.VMEM((A_DK, A_DV), F32),
        ],
        compiler_params=pltpu.CompilerParams(
            dimension_semantics=("arbitrary", "arbitrary", "arbitrary"),
            vmem_limit_bytes=VMEM_LIMIT),
        name="gdn",
    )(proj, proj, proj, proj, small_t, alog_b, dtb_b, conv_a, conv_a, conv_a, norm_a)


def _rotary(x, cos, sin):
    half = x.shape[-1] // 2
    x1, x2 = x[:, :half], x[:, half:]
    return jnp.concatenate([x1 * cos - x2 * sin, x1 * sin + x2 * cos], axis=1)


def _ret_kernel(q_ref, k_ref, v_ref, g_ref, cos_ref, sin_ref, nb_ref, o_ref, state_ref):
    h = pl.program_id(1)
    r = pl.program_id(2)
    lb = q_ref.shape[0]
    nsb = lb // SB

    @pl.when(r == 0)
    def _():
        state_ref[...] = jnp.zeros_like(state_ref)

    hf = jnp.full((SB, SB), h, jnp.int32).astype(F32)
    lg = jnp.log1p(-jnp.exp2(-5.0 - hf))
    row = lax.broadcasted_iota(jnp.int32, (SB, SB), 0)
    col = lax.broadcasted_iota(jnp.int32, (SB, SB), 1)
    allow = ((col >> 6) <= (row >> 6)).astype(F32)
    dist = jnp.abs(row - col).astype(F32)
    mask = jnp.exp(lg * dist) * allow
    rowf = row.astype(F32)
    q_dec = jnp.exp(lg * (rowf + 1.0))
    k_dec = jnp.exp(lg * (SB - 1.0 - rowf)) * (B_DK ** -0.5)
    sb_dec = jnp.exp(lg[0:1, 0:1] * SB)
    nb = nb_ref[...]

    def sb_body(s, carry):
        r0 = pl.multiple_of(s * SB, SB)
        cos = cos_ref[pl.ds(r0, SB), :]
        sin = sin_ref[pl.ds(r0, SB), :]
        q = _rotary(q_ref[pl.ds(r0, SB), :].astype(F32), cos, sin)
        k = _rotary(k_ref[pl.ds(r0, SB), :].astype(F32), cos, sin)
        vb = v_ref[pl.ds(r0, SB), :]
        sc = lax.dot_general(q.astype(BF16), (k * (B_DK ** -0.5)).astype(BF16), _NT,
                             preferred_element_type=F32)
        state = state_ref[...]
        o = _dot((sc * mask).astype(BF16), vb) + _dot((q * q_dec).astype(BF16), state.astype(BF16))
        state_ref[...] = state * sb_dec + lax.dot_general(
            (k * k_dec).astype(BF16), vb, _TN, preferred_element_type=F32)
        ms = jnp.mean(o * o, axis=-1, keepdims=True)
        g = g_ref[pl.ds(r0, SB), :].astype(F32)
        o_ref[pl.ds(r0, SB), :] = (o * lax.rsqrt(ms + EPS) * nb * _silu(g)).astype(o_ref.dtype)
        return carry

    lax.fori_loop(0, nsb, sb_body, 0)


def _ret(proj, cos, sin, norm_b, batch, seq, lb):
    t = proj.shape[0]
    nr = seq // lb
    base = 2 * A_QK + 2 * A_V
    q_off = base // B_DK
    k_off = (base + B_QK) // B_DK
    v_off = (base + 2 * B_QK) // B_DV
    g_off = (base + 2 * B_QK + B_V) // B_DV

    def rows(b, h, r):
        return b * nr + r

    return pl.pallas_call(
        _ret_kernel,
        grid=(batch, B_HEADS, nr),
        in_specs=[
            pl.BlockSpec((lb, B_DK), lambda b, h, r: (rows(b, h, r), q_off + h)),
            pl.BlockSpec((lb, B_DK), lambda b, h, r: (rows(b, h, r), k_off + h)),
            pl.BlockSpec((lb, B_DV), lambda b, h, r: (rows(b, h, r), v_off + h)),
            pl.BlockSpec((lb, B_DV), lambda b, h, r: (rows(b, h, r), g_off + h)),
            pl.BlockSpec((lb, B_DK // 2), lambda b, h, r: (r, 0)),
            pl.BlockSpec((lb, B_DK // 2), lambda b, h, r: (r, 0)),
            pl.BlockSpec((1, B_DV), lambda b, h, r: (0, h)),
        ],
        out_specs=pl.BlockSpec((lb, B_DV), lambda b, h, r: (rows(b, h, r), h)),
        out_shape=jax.ShapeDtypeStruct((t, B_V), BF16),
        scratch_shapes=[pltpu.VMEM((B_DK, B_DV), F32)],
        compiler_params=pltpu.CompilerParams(
            dimension_semantics=("arbitrary", "arbitrary", "arbitrary"),
            vmem_limit_bytes=VMEM_LIMIT),
        name="ret",
    )(proj, proj, proj, proj, cos, sin, norm_b)


def _merge_kernel(oa_ref, ob_ref, ga_ref, gb_ref, x_ref, wa_ref, wb_ref, wo_ref, n_ref, o_ref):
    ya = _dot(oa_ref[...], wa_ref[...])
    yb = _dot(ob_ref[...], wb_ref[...])
    ga = jax.nn.sigmoid(ga_ref[...].astype(F32))
    gb = jax.nn.sigmoid(gb_ref[...].astype(F32))
    mix = _dot((ga * ya + gb * yb).astype(BF16), wo_ref[...])
    ms = jnp.mean(mix * mix, axis=-1, keepdims=True)
    o_ref[...] = x_ref[...] + mix * lax.rsqrt(ms + EPS) * n_ref[...]


def _merge(oa, ob, proj, x2, wa, wb, wo, gain, tm):
    t, d = x2.shape
    gate_off = (2 * A_QK + 2 * A_V + 2 * B_QK + 2 * B_V) // d
    const = lambda i: (0, 0)
    return pl.pallas_call(
        _merge_kernel,
        grid=(t // tm,),
        in_specs=[
            pl.BlockSpec((tm, A_V), lambda i: (i, 0)),
            pl.BlockSpec((tm, B_V), lambda i: (i, 0)),
            pl.BlockSpec((tm, d), lambda i: (i, gate_off)),
            pl.BlockSpec((tm, d), lambda i: (i, gate_off + 1)),
            pl.BlockSpec((tm, d), lambda i: (i, 0)),
            pl.BlockSpec((A_V, d), const),
            pl.BlockSpec((B_V, d), const),
            pl.BlockSpec((d, d), const),
            pl.BlockSpec((1, d), const),
        ],
        out_specs=pl.BlockSpec((tm, d), lambda i: (i, 0)),
        out_shape=jax.ShapeDtypeStruct((t, d), F32),
        compiler_params=pltpu.CompilerParams(
            dimension_semantics=("arbitrary",), vmem_limit_bytes=VMEM_LIMIT),
        name="merge",
    )(oa, ob, proj, proj, x2, wa, wb, wo, gain)


def _mlp_kernel(x_ref, gpre_ref, wu_ref, wd_ref, gpost_ref, o_ref, *, ff_tile):
    x = x_ref[...]
    ms = jnp.mean(x * x, axis=-1, keepdims=True)
    hn = (x * lax.rsqrt(ms + EPS) * gpre_ref[...]).astype(BF16)
    d_ff = wu_ref.shape[1]
    ff = jnp.zeros(x.shape, F32)
    for j in range(d_ff // ff_tile):
        up = _dot(hn, wu_ref[:, j * ff_tile:(j + 1) * ff_tile])
        act = jnp.square(jnp.maximum(up, 0.0)).astype(BF16)
        ff = ff + _dot(act, wd_ref[j * ff_tile:(j + 1) * ff_tile, :])
    ms2 = jnp.mean(ff * ff, axis=-1, keepdims=True)
    o_ref[...] = x + ff * lax.rsqrt(ms2 + EPS) * gpost_ref[...]


def _mlp(x1, gpre, wu, wd, gpost, tm, ff_tile):
    t, d = x1.shape
    d_ff = wu.shape[1]
    const = lambda i: (0, 0)
    return pl.pallas_call(
        functools.partial(_mlp_kernel, ff_tile=ff_tile),
        grid=(t // tm,),
        in_specs=[
            pl.BlockSpec((tm, d), lambda i: (i, 0)),
            pl.BlockSpec((1, d), const),
            pl.BlockSpec((d, d_ff), const),
            pl.BlockSpec((d_ff, d), const),
            pl.BlockSpec((1, d), const),
        ],
        out_specs=pl.BlockSpec((tm, d), lambda i: (i, 0)),
        out_shape=jax.ShapeDtypeStruct((t, d), F32),
        compiler_params=pltpu.CompilerParams(
            dimension_semantics=("arbitrary",), vmem_limit_bytes=VMEM_LIMIT),
        name="mlp",
    )(x1, gpre, wu, wd, gpost)


def _pick(n, candidates):
    for c in candidates:
        if n % c == 0:
            return c
    raise ValueError(f"no tile for {n} among {candidates}")


def _layer(x, n_mix_pre, n_mix_post, n_mlp_pre, n_mlp_post, w_in, conv_a, a_log, dt_bias,
           norm_a, norm_b, w_br_a, w_br_b, w_out, w_up, w_down):
    batch, seq, d = x.shape
    t = batch * seq
    assert seq % SB == 0
    x2 = x.reshape(t, d)

    n_small_lo = 2 * A_QK + 2 * A_V
    n_small_hi = n_small_lo + 2 * A_HEADS
    w_main = jnp.concatenate([w_in[:, :n_small_lo], w_in[:, n_small_hi:]], axis=1).astype(BF16)
    w_small = jnp.pad(w_in[:, n_small_lo:n_small_hi], ((0, 0), (0, LANES - 2 * A_HEADS))).astype(BF16)

    tm = _pick(t, (1024, 512, 256))
    tn = _pick(w_main.shape[1], (2048, 1024, 512))
    proj, small = _inproj(x2, n_mix_pre.reshape(1, d), w_main, w_small, tm, tn)

    lb = _pick(seq, (1024, 512, 256))
    small_t = jnp.transpose(small[:, :2 * A_HEADS].reshape(batch, seq, 2 * A_HEADS), (0, 2, 1))
    alog_b = jnp.broadcast_to(a_log.reshape(A_HEADS, 1), (A_HEADS, lb))
    dtb_b = jnp.broadcast_to(dt_bias.reshape(A_HEADS, 1), (A_HEADS, lb))
    oa = _gdn(proj, small_t, alog_b, dtb_b, conv_a, norm_a.reshape(1, A_DV), batch, seq, lb)

    inv_freq = ROPE_BASE ** (-jnp.arange(0, B_DK, 2, dtype=F32) / B_DK)
    ang = jnp.arange(seq, dtype=F32)[:, None] * inv_freq[None, :]
    ob = _ret(proj, jnp.cos(ang), jnp.sin(ang), norm_b.reshape(1, B_V), batch, seq, lb)

    tm2 = _pick(t, (512, 256))
    x1 = _merge(oa, ob, proj, x2, w_br_a.astype(BF16), w_br_b.astype(BF16), w_out.astype(BF16),
                n_mix_post.reshape(1, d), tm2)
    out = _mlp(x1, n_mlp_pre.reshape(1, d), w_up.astype(BF16), w_down.astype(BF16),
               n_mlp_post.reshape(1, d), tm2, 1024)
    return out.reshape(batch, seq, d)


def kernel(x, norm_mix_pre, norm_mix_post, norm_mlp_pre, norm_mlp_post, w_in, conv_a, a_log,
           dt_bias, norm_a, norm_b, w_br_a, w_br_b, w_out, w_up, w_down):
    for l in range(w_in.shape[0]):
        x = _layer(x, norm_mix_pre[l], norm_mix_post[l], norm_mlp_pre[l], norm_mlp_post[l],
                   w_in[l], conv_a[l], a_log[l], dt_bias[l], norm_a[l], norm_b[l],
                   w_br_a[l], w_br_b[l], w_out[l], w_up[l], w_down[l])
    return x
```

```python
import functools

import jax
import jax.numpy as jnp
from jax import lax
from jax.experimental import pallas as pl
from jax.experimental.pallas import tpu as pltpu

F32 = jnp.float32
BF16 = jnp.bfloat16

CHUNK = 64
EPS = 1e-6
A_HEADS, A_DK, A_DV, CONV_W = 8, 128, 256, 4
B_HEADS, B_DK, B_DV = 4, 256, 512
ROPE_BASE = 10000.0
A_QK, A_V = A_HEADS * A_DK, A_HEADS * A_DV
B_QK, B_V = B_HEADS * B_DK, B_HEADS * B_DV

SB = 256
CPS = SB // CHUNK
LANES = 128
HG = 4
VMEM_LIMIT = 56 * 1024 * 1024

_NT = (((1,), (1,)), ((), ()))
_TN = (((0,), (0,)), ((), ()))


def _dot(a, b):
    return jnp.dot(a, b, preferred_element_type=F32)


def _silu(x):
    return x * jax.nn.sigmoid(x)


def _inproj_kernel(x_ref, g_ref, w_ref, ws_ref, o_ref, os_ref, xn_ref):
    @pl.when(pl.program_id(1) == 0)
    def _():
        x = x_ref[...]
        ms = jnp.mean(x * x, axis=-1, keepdims=True)
        xn = (x * lax.rsqrt(ms + EPS) * g_ref[...]).astype(BF16)
        xn_ref[...] = xn
        os_ref[...] = _dot(xn, ws_ref[...])

    o_ref[...] = _dot(xn_ref[...], w_ref[...]).astype(o_ref.dtype)


def _inproj(x2, gain, w_main, w_small, tm, tn):
    t, d = x2.shape
    n = w_main.shape[1]
    return pl.pallas_call(
        _inproj_kernel,
        grid=(t // tm, n // tn),
        in_specs=[
            pl.BlockSpec((tm, d), lambda i, j: (i, 0)),
            pl.BlockSpec((1, d), lambda i, j: (0, 0)),
            pl.BlockSpec((d, tn), lambda i, j: (0, j)),
            pl.BlockSpec((d, LANES), lambda i, j: (0, 0)),
        ],
        out_specs=[
            pl.BlockSpec((tm, tn), lambda i, j: (i, j)),
            pl.BlockSpec((tm, LANES), lambda i, j: (i, 0)),
        ],
        out_shape=[
            jax.ShapeDtypeStruct((t, n), BF16),
            jax.ShapeDtypeStruct((t, LANES), F32),
        ],
        scratch_shapes=[pltpu.VMEM((tm, d), BF16)],
        compiler_params=pltpu.CompilerParams(
            dimension_semantics=("arbitrary", "arbitrary"), vmem_limit_bytes=VMEM_LIMIT),
        name="inproj",
    )(x2, gain, w_main, w_small)


def _conv_silu(e_ref, x_ref, c_ref, lb):
    e_ref[pl.ds(8, lb), :] = x_ref[...].astype(F32)
    acc = e_ref[pl.ds(8 - (CONV_W - 1), lb), :] * c_ref[0:1, :]
    for j in range(1, CONV_W):
        acc = acc + e_ref[pl.ds(8 - (CONV_W - 1) + j, lb), :] * c_ref[j:j + 1, :]
    e_ref[pl.ds(0, 8), :] = e_ref[pl.ds(lb, 8), :]
    return _silu(acc)


def _l2norm(y):
    return y * lax.rsqrt(jnp.sum(y * y, axis=-1, keepdims=True) + EPS)


def _group_cumsum_rows(x, group):
    lane = lax.broadcasted_iota(jnp.int32, x.shape, 1)
    pos = lane & (group - 1)
    s = 1
    while s < group:
        shifted = pltpu.roll(x, s, axis=1)
        x = x + jnp.where(pos >= s, shifted, 0.0)
        s *= 2
    return x


def _col_replicated(row):
    return jnp.transpose(jnp.broadcast_to(row, (LANES, SB)))


def _gdn_kernel(q_ref, k_ref, v_ref, z_ref, st_ref, alog_ref, dtb_ref, cq_ref, ck_ref, cv_ref,
                na_ref, o_ref, eq_ref, ek_ref, ev_ref, state_ref):
    grp = pl.program_id(1)
    r = pl.program_id(2)
    heads = range(HG)

    @pl.when(r == 0)
    def _():
        eq_ref[pl.ds(0, 8), :] = jnp.zeros((8, HG * A_DK), F32)
        ek_ref[pl.ds(0, 8), :] = jnp.zeros((8, HG * A_DK), F32)
        ev_ref[pl.ds(0, 8), :] = jnp.zeros((8, HG * A_DV), F32)
        state_ref[...] = jnp.zeros_like(state_ref)

    qa = _conv_silu(eq_ref, q_ref, cq_ref, SB)
    ka = _conv_silu(ek_ref, k_ref, ck_ref, SB)
    va = _conv_silu(ev_ref, v_ref, cv_ref, SB)
    q = [_l2norm(qa[:, j * A_DK:(j + 1) * A_DK]) * (A_DK ** -0.5) for j in heads]
    k = [_l2norm(ka[:, j * A_DK:(j + 1) * A_DK]) for j in heads]
    v = [va[:, j * A_DV:(j + 1) * A_DV] for j in heads]

    gr, gc, bc = [], [], []
    for j in heads:
        hh = grp * HG + j
        beta_row = jax.nn.sigmoid(st_ref[pl.ds(hh, 1), :])
        xg = st_ref[pl.ds(A_HEADS + hh, 1), :] + dtb_ref[pl.ds(hh, 1), :]
        softplus = jnp.maximum(xg, 0.0) + jnp.log1p(jnp.exp(-jnp.abs(xg)))
        g_row = -jnp.exp(alog_ref[pl.ds(hh, 1), :]) * softplus
        gc_row = _group_cumsum_rows(jnp.broadcast_to(g_row, (8, SB)), CHUNK)[0:1]
        gr.append(gc_row)
        gc.append(_col_replicated(gc_row))
        bc.append(_col_replicated(beta_row))

    row = lax.broadcasted_iota(jnp.int32, (SB, SB), 0)
    col = lax.broadcasted_iota(jnp.int32, (SB, SB), 1)

    def same(bits):
        return ((row >> bits) == (col >> bits)).astype(F32)

    m16, m32, m64 = same(4), same(5), same(6)
    lower = (row >= col).astype(F32)
    causal_f = m64 * lower
    strict_f = m64 * (row > col).astype(F32)
    eye = (row == col).astype(F32)
    c1_mask = (m32 - m16) * lower
    c2_mask = (m64 - m32) * lower
    rsel = lax.broadcasted_iota(jnp.int32, (CPS * A_DK, SB), 0)
    csel = lax.broadcasted_iota(jnp.int32, (CPS * A_DK, SB), 1)
    chunk_sel = ((rsel >> 7) == (csel >> 6)).astype(F32)

    decay = [jnp.exp(jnp.minimum(jnp.concatenate([gc[j], gc[j]], axis=1) - gr[j], 0.0)) * causal_f
             for j in heads]
    kb = [k[j] * bc[j] for j in heads]
    kq = [lax.dot_general(jnp.concatenate([kb[j], q[j]], axis=0).astype(BF16), k[j].astype(BF16),
                          _NT, preferred_element_type=F32) for j in heads]
    a = [kq[j][:SB] * decay[j] * strict_f for j in heads]
    attn = [(kq[j][SB:] * decay[j]).astype(BF16) for j in heads]

    n1 = [-(a[j] * m16) for j in heads]
    n1b = [n1[j].astype(BF16) for j in heads]
    n2b = [_dot(n1b[j], n1b[j]).astype(BF16) for j in heads]
    n4b = [_dot(n2b[j], n2b[j]).astype(BF16) for j in heads]
    n8b = [_dot(n4b[j], n4b[j]).astype(BF16) for j in heads]
    p = [eye + n1[j] for j in heads]
    p = [p[j] + _dot(p[j].astype(BF16), n2b[j]) for j in heads]
    p = [p[j] + _dot(p[j].astype(BF16), n4b[j]) for j in heads]
    p = [p[j] + _dot(p[j].astype(BF16), n8b[j]) for j in heads]
    pb = [p[j].astype(BF16) for j in heads]
    x1 = [_dot(pb[j], (a[j] * c1_mask).astype(BF16)).astype(BF16) for j in heads]
    t32 = [p[j] - _dot(x1[j], pb[j]) for j in heads]
    t32b = [t32[j].astype(BF16) for j in heads]
    x2 = [_dot(t32b[j], (a[j] * c2_mask).astype(BF16)).astype(BF16) for j in heads]
    tinv = [(t32[j] - _dot(x2[j], t32b[j])).astype(BF16) for j in heads]

    eg = [jnp.exp(gc[j]) for j in heads]
    rhs = [jnp.concatenate([v[j] * jnp.concatenate([bc[j], bc[j]], axis=1), kb[j] * eg[j]],
                           axis=1).astype(BF16) for j in heads]
    uwb = [_dot(tinv[j], rhs[j]).astype(BF16) for j in heads]
    aw = [_dot(attn[j], uwb[j]) for j in heads]
    qeff = [(q[j] * eg[j] - aw[j][:, A_DV:]).astype(BF16) for j in heads]

    glast = [[gc[j][c * CHUNK + CHUNK - 1:(c + 1) * CHUNK, :] for c in range(CPS)] for j in heads]
    bp = []
    for j in heads:
        glast_rep = jnp.concatenate(
            [jnp.broadcast_to(g, (CHUNK, LANES)) for g in glast[j]], axis=0)
        kd_t = jnp.transpose(k[j] * jnp.exp(glast_rep - gc[j]))
        kd_sel = (jnp.concatenate([kd_t] * CPS, axis=0) * chunk_sel).astype(BF16)
        bp.append(_dot(kd_sel, uwb[j]))

    st = [state_ref[j] for j in heads]
    outs = [[] for _ in heads]
    for c in range(CPS):
        for j in heads:
            blk = bp[j][c * A_DK:(c + 1) * A_DK]
            lhs = jnp.concatenate([blk[:, A_DV:].astype(BF16), qeff[j][c * CHUNK:(c + 1) * CHUNK]],
                                  axis=0)
            res = _dot(lhs, st[j].astype(BF16))
            outs[j].append(res[A_DK:])
            ld = jnp.exp(glast[j][c])
            st[j] = st[j] * jnp.concatenate([ld, ld], axis=1) + blk[:, :A_DV] - res[:A_DK]

    na = na_ref[...]
    for j in heads:
        state_ref[j] = st[j]
        o = jnp.concatenate(outs[j], axis=0) + aw[j][:, :A_DV]
        ms = jnp.mean(o * o, axis=-1, keepdims=True)
        z = z_ref[:, j * A_DV:(j + 1) * A_DV].astype(F32)
        o_ref[:, j * A_DV:(j + 1) * A_DV] = (
            o * lax.rsqrt(ms + EPS) * na * _silu(z)).astype(o_ref.dtype)


def _gdn(proj, small_t, alog_b, dtb_b, conv_a, norm_a, batch, seq):
    t = proj.shape[0]
    nr = seq // SB
    wq, wv = HG * A_DK, HG * A_DV
    k_off = A_QK // wq
    v_off = (2 * A_QK) // wv
    z_off = (2 * A_QK + A_V) // wv

    def rows(b, g, r):
        return b * nr + r

    return pl.pallas_call(
        _gdn_kernel,
        grid=(batch, A_HEADS // HG, nr),
        in_specs=[
            pl.BlockSpec((SB, wq), lambda b, g, r: (rows(b, g, r), g)),
            pl.BlockSpec((SB, wq), lambda b, g, r: (rows(b, g, r), k_off + g)),
            pl.BlockSpec((SB, wv), lambda b, g, r: (rows(b, g, r), v_off + g)),
            pl.BlockSpec((SB, wv), lambda b, g, r: (rows(b, g, r), z_off + g)),
            pl.BlockSpec((None, 2 * A_HEADS, SB), lambda b, g, r: (b, 0, r)),
            pl.BlockSpec((A_HEADS, SB), lambda b, g, r: (0, 0)),
            pl.BlockSpec((A_HEADS, SB), lambda b, g, r: (0, 0)),
            pl.BlockSpec((CONV_W, wq), lambda b, g, r: (0, g)),
            pl.BlockSpec((CONV_W, wq), lambda b, g, r: (0, k_off + g)),
            pl.BlockSpec((CONV_W, wv), lambda b, g, r: (0, v_off + g)),
            pl.BlockSpec((1, A_DV), lambda b, g, r: (0, 0)),
        ],
        out_specs=pl.BlockSpec((SB, wv), lambda b, g, r: (rows(b, g, r), g)),
        out_shape=jax.ShapeDtypeStruct((t, A_V), BF16),
        scratch_shapes=[
            pltpu.VMEM((SB + 8, wq), F32),
            pltpu.VMEM((SB + 8, wq), F32),
            pltpu.VMEM((SB + 8, wv), F32),
            pltpu.VMEM((HG, A_DK, A_DV), F32),
        ],
        compiler_params=pltpu.CompilerParams(
            dimension_semantics=("arbitrary", "arbitrary", "arbitrary"),
            vmem_limit_bytes=VMEM_LIMIT),
        name="gdn",
    )(proj, proj, proj, proj, small_t, alog_b, dtb_b, conv_a, conv_a, conv_a, norm_a)


def _rotary(x, cos, sin):
    half = x.shape[-1] // 2
    x1, x2 = x[:, :half], x[:, half:]
    return jnp.concatenate([x1 * cos - x2 * sin, x1 * sin + x2 * cos], axis=1)


def _ret_kernel(q_ref, k_ref, v_ref, g_ref, cos_ref, sin_ref, nb_ref, o_ref, state_ref):
    h = pl.program_id(1)
    r = pl.program_id(2)
    lb = q_ref.shape[0]
    nsb = lb // SB

    @pl.when(r == 0)
    def _():
        state_ref[...] = jnp.zeros_like(state_ref)

    hf = jnp.full((SB, SB), h, jnp.int32).astype(F32)
    lg = jnp.log1p(-jnp.exp2(-5.0 - hf))
    row = lax.broadcasted_iota(jnp.int32, (SB, SB), 0)
    col = lax.broadcasted_iota(jnp.int32, (SB, SB), 1)
    allow = ((col >> 6) <= (row >> 6)).astype(F32)
    dist = jnp.abs(row - col).astype(F32)
    mask = jnp.exp(lg * dist) * allow
    rowf = row.astype(F32)
    q_dec = jnp.exp(lg * (rowf + 1.0))
    k_dec = jnp.exp(lg * (SB - 1.0 - rowf)) * (B_DK ** -0.5)
    sb_dec = jnp.exp(lg[0:1, 0:1] * SB)
    nb = nb_ref[...]

    state = state_ref[...]
    for s in range(nsb):
        r0 = s * SB
        cos = cos_ref[pl.ds(r0, SB), :]
        sin = sin_ref[pl.ds(r0, SB), :]
        q = _rotary(q_ref[pl.ds(r0, SB), :].astype(F32), cos, sin)
        k = _rotary(k_ref[pl.ds(r0, SB), :].astype(F32), cos, sin)
        vb = v_ref[pl.ds(r0, SB), :]
        sc = lax.dot_general(q.astype(BF16), (k * (B_DK ** -0.5)).astype(BF16), _NT,
                             preferred_element_type=F32)
        o = _dot((sc * mask).astype(BF16), vb) + _dot((q * q_dec).astype(BF16), state.astype(BF16))
        state = state * sb_dec + lax.dot_general(
            (k * k_dec).astype(BF16), vb, _TN, preferred_element_type=F32)
        ms = jnp.mean(o * o, axis=-1, keepdims=True)
        g = g_ref[pl.ds(r0, SB), :].astype(F32)
        o_ref[pl.ds(r0, SB), :] = (o * lax.rsqrt(ms + EPS) * nb * _silu(g)).astype(o_ref.dtype)
    state_ref[...] = state


def _ret(proj, cos, sin, norm_b, batch, seq, lb):
    t = proj.shape[0]
    nr = seq // lb
    base = 2 * A_QK + 2 * A_V
    q_off = base // B_DK
    k_off = (base + B_QK) // B_DK
    v_off = (base + 2 * B_QK) // B_DV
    g_off = (base + 2 * B_QK + B_V) // B_DV

    def rows(b, h, r):
        return b * nr + r

    return pl.pallas_call(
        _ret_kernel,
        grid=(batch, B_HEADS, nr),
        in_specs=[
            pl.BlockSpec((lb, B_DK), lambda b, h, r: (rows(b, h, r), q_off + h)),
            pl.BlockSpec((lb, B_DK), lambda b, h, r: (rows(b, h, r), k_off + h)),
            pl.BlockSpec((lb, B_DV), lambda b, h, r: (rows(b, h, r), v_off + h)),
            pl.BlockSpec((lb, B_DV), lambda b, h, r: (rows(b, h, r), g_off + h)),
            pl.BlockSpec((lb, B_DK // 2), lambda b, h, r: (r, 0)),
            pl.BlockSpec((lb, B_DK // 2), lambda b, h, r: (r, 0)),
            pl.BlockSpec((1, B_DV), lambda b, h, r: (0, h)),
        ],
        out_specs=pl.BlockSpec((lb, B_DV), lambda b, h, r: (rows(b, h, r), h)),
        out_shape=jax.ShapeDtypeStruct((t, B_V), BF16),
        scratch_shapes=[pltpu.VMEM((B_DK, B_DV), F32)],
        compiler_params=pltpu.CompilerParams(
            dimension_semantics=("arbitrary", "arbitrary", "arbitrary"),
            vmem_limit_bytes=VMEM_LIMIT),
        name="ret",
    )(proj, proj, proj, proj, cos, sin, norm_b)


def _merge_kernel(oa_ref, ob_ref, ga_ref, gb_ref, x_ref, wa_ref, wb_ref, wo_ref, n_ref, o_ref):
    ya = _dot(oa_ref[...], wa_ref[...])
    yb = _dot(ob_ref[...], wb_ref[...])
    ga = jax.nn.sigmoid(ga_ref[...].astype(F32))
    gb = jax.nn.sigmoid(gb_ref[...].astype(F32))
    mix = _dot((ga * ya + gb * yb).astype(BF16), wo_ref[...])
    ms = jnp.mean(mix * mix, axis=-1, keepdims=True)
    o_ref[...] = x_ref[...] + mix * lax.rsqrt(ms + EPS) * n_ref[...]


def _merge(oa, ob, proj, x2, wa, wb, wo, gain, tm):
    t, d = x2.shape
    gate_off = (2 * A_QK + 2 * A_V + 2 * B_QK + 2 * B_V) // d
    const = lambda i: (0, 0)
    return pl.pallas_call(
        _merge_kernel,
        grid=(t // tm,),
        in_specs=[
            pl.BlockSpec((tm, A_V), lambda i: (i, 0)),
            pl.BlockSpec((tm, B_V), lambda i: (i, 0)),
            pl.BlockSpec((tm, d), lambda i: (i, gate_off)),
            pl.BlockSpec((tm, d), lambda i: (i, gate_off + 1)),
            pl.BlockSpec((tm, d), lambda i: (i, 0)),
            pl.BlockSpec((A_V, d), const),
            pl.BlockSpec((B_V, d), const),
            pl.BlockSpec((d, d), const),
            pl.BlockSpec((1, d), const),
        ],
        out_specs=pl.BlockSpec((tm, d), lambda i: (i, 0)),
        out_shape=jax.ShapeDtypeStruct((t, d), F32),
        compiler_params=pltpu.CompilerParams(
            dimension_semantics=("arbitrary",), vmem_limit_bytes=VMEM_LIMIT),
        name="merge",
    )(oa, ob, proj, proj, x2, wa, wb, wo, gain)


def _mlp_kernel(x_ref, gpre_ref, wu_ref, wd_ref, gpost_ref, o_ref, *, ff_tile):
    x = x_ref[...]
    ms = jnp.mean(x * x, axis=-1, keepdims=True)
    hn = (x * lax.rsqrt(ms + EPS) * gpre_ref[...]).astype(BF16)
    d_ff = wu_ref.shape[1]
    ff = jnp.zeros(x.shape, F32)
    for j in range(d_ff // ff_tile):
        up = _dot(hn, wu_ref[:, j * ff_tile:(j + 1) * ff_tile])
        act = jnp.square(jnp.maximum(up, 0.0)).astype(BF16)
        ff = ff + _dot(act, wd_ref[j * ff_tile:(j + 1) * ff_tile, :])
    ms2 = jnp.mean(ff * ff, axis=-1, keepdims=True)
    o_ref[...] = x + ff * lax.rsqrt(ms2 + EPS) * gpost_ref[...]


def _mlp(x1, gpre, wu, wd, gpost, tm, ff_tile):
    t, d = x1.shape
    d_ff = wu.shape[1]
    const = lambda i: (0, 0)
    return pl.pallas_call(
        functools.partial(_mlp_kernel, ff_tile=ff_tile),
        grid=(t // tm,),
        in_specs=[
            pl.BlockSpec((tm, d), lambda i: (i, 0)),
            pl.BlockSpec((1, d), const),
            pl.BlockSpec((d, d_ff), const),
            pl.BlockSpec((d_ff, d), const),
            pl.BlockSpec((1, d), const),
        ],
        out_specs=pl.BlockSpec((tm, d), lambda i: (i, 0)),
        out_shape=jax.ShapeDtypeStruct((t, d), F32),
        compiler_params=pltpu.CompilerParams(
            dimension_semantics=("arbitrary",), vmem_limit_bytes=VMEM_LIMIT),
        name="mlp",
    )(x1, gpre, wu, wd, gpost)


def _pick(n, candidates):
    for c in candidates:
        if n % c == 0:
            return c
    raise ValueError(f"no tile for {n} among {candidates}")


def _layer(x, n_mix_pre, n_mix_post, n_mlp_pre, n_mlp_post, w_in, conv_a, a_log, dt_bias,
           norm_a, norm_b, w_br_a, w_br_b, w_out, w_up, w_down):
    batch, seq, d = x.shape
    t = batch * seq
    assert seq % SB == 0
    x2 = x.reshape(t, d)

    n_small_lo = 2 * A_QK + 2 * A_V
    n_small_hi = n_small_lo + 2 * A_HEADS
    w_main = jnp.concatenate([w_in[:, :n_small_lo], w_in[:, n_small_hi:]], axis=1).astype(BF16)
    w_small = jnp.pad(w_in[:, n_small_lo:n_small_hi], ((0, 0), (0, LANES - 2 * A_HEADS))).astype(BF16)

    tm = _pick(t, (1024, 512, 256))
    tn = _pick(w_main.shape[1], (2048, 1024, 512))
    proj, small = _inproj(x2, n_mix_pre.reshape(1, d), w_main, w_small, tm, tn)

    small_t = jnp.transpose(small[:, :2 * A_HEADS].reshape(batch, seq, 2 * A_HEADS), (0, 2, 1))
    alog_b = jnp.broadcast_to(a_log.reshape(A_HEADS, 1), (A_HEADS, SB))
    dtb_b = jnp.broadcast_to(dt_bias.reshape(A_HEADS, 1), (A_HEADS, SB))
    oa = _gdn(proj, small_t, alog_b, dtb_b, conv_a, norm_a.reshape(1, A_DV), batch, seq)

    lb = _pick(seq, (1024, 512, 256))
    inv_freq = ROPE_BASE ** (-jnp.arange(0, B_DK, 2, dtype=F32) / B_DK)
    ang = jnp.arange(seq, dtype=F32)[:, None] * inv_freq[None, :]
    ob = _ret(proj, jnp.cos(ang), jnp.sin(ang), norm_b.reshape(1, B_V), batch, seq, lb)

    tm2 = _pick(t, (512, 256))
    x1 = _merge(oa, ob, proj, x2, w_br_a.astype(BF16), w_br_b.astype(BF16), w_out.astype(BF16),
                n_mix_post.reshape(1, d), tm2)
    out = _mlp(x1, n_mlp_pre.reshape(1, d), w_up.astype(BF16), w_down.astype(BF16),
               n_mlp_post.reshape(1, d), tm2, 1024)
    return out.reshape(batch, seq, d)


def kernel(x, norm_mix_pre, norm_mix_post, norm_mlp_pre, norm_mlp_post, w_in, conv_a, a_log,
           dt_bias, norm_a, norm_b, w_br_a, w_br_b, w_out, w_up, w_down):
    for l in range(w_in.shape[0]):
        x = _layer(x, norm_mix_pre[l], norm_mix_post[l], norm_mlp_pre[l], norm_mlp_post[l],
                   w_in[l], conv_a[l], a_log[l], dt_bias[l], norm_a[l], norm_b[l],
                   w_br_a[l], w_br_b[l], w_out[l], w_up[l], w_down[l])
    return x
```

```python
import functools

import jax
import jax.numpy as jnp
from jax import lax
from jax.experimental import pallas as pl
from jax.experimental.pallas import tpu as pltpu

F32 = jnp.float32
BF16 = jnp.bfloat16

CHUNK = 64
EPS = 1e-6
A_HEADS, A_DK, A_DV, CONV_W = 8, 128, 256, 4
B_HEADS, B_DK, B_DV = 4, 256, 512
ROPE_BASE = 10000.0
A_QK, A_V = A_HEADS * A_DK, A_HEADS * A_DV
B_QK, B_V = B_HEADS * B_DK, B_HEADS * B_DV

SB = 256
CPS = SB // CHUNK
LANES = 128
HG = 4
VMEM_LIMIT = 56 * 1024 * 1024

_NT = (((1,), (1,)), ((), ()))
_TN = (((0,), (0,)), ((), ()))


def _dot(a, b):
    return jnp.dot(a, b, preferred_element_type=F32)


def _silu(x):
    return x * jax.nn.sigmoid(x)


def _inproj_kernel(x_ref, g_ref, w_ref, ws_ref, o_ref, os_ref, xn_ref):
    @pl.when(pl.program_id(1) == 0)
    def _():
        x = x_ref[...]
        ms = jnp.mean(x * x, axis=-1, keepdims=True)
        xn = (x * lax.rsqrt(ms + EPS) * g_ref[...]).astype(BF16)
        xn_ref[...] = xn
        os_ref[...] = _dot(xn, ws_ref[...])

    o_ref[...] = _dot(xn_ref[...], w_ref[...]).astype(o_ref.dtype)


def _inproj(x2, gain, w_main, w_small, tm, tn):
    t, d = x2.shape
    n = w_main.shape[1]
    return pl.pallas_call(
        _inproj_kernel,
        grid=(t // tm, n // tn),
        in_specs=[
            pl.BlockSpec((tm, d), lambda i, j: (i, 0)),
            pl.BlockSpec((1, d), lambda i, j: (0, 0)),
            pl.BlockSpec((d, tn), lambda i, j: (0, j)),
            pl.BlockSpec((d, LANES), lambda i, j: (0, 0)),
        ],
        out_specs=[
            pl.BlockSpec((tm, tn), lambda i, j: (i, j)),
            pl.BlockSpec((tm, LANES), lambda i, j: (i, 0)),
        ],
        out_shape=[
            jax.ShapeDtypeStruct((t, n), BF16),
            jax.ShapeDtypeStruct((t, LANES), F32),
        ],
        scratch_shapes=[pltpu.VMEM((tm, d), BF16)],
        compiler_params=pltpu.CompilerParams(
            dimension_semantics=("arbitrary", "arbitrary"), vmem_limit_bytes=VMEM_LIMIT),
        name="inproj",
    )(x2, gain, w_main, w_small)


def _conv_silu(tail_ref, x_ref, c_ref):
    x = x_ref[...].astype(F32)
    rows = x.shape[0]
    xe = jnp.concatenate([tail_ref[...], x], axis=0)
    tail_ref[...] = x[rows - 8:, :]
    acc = x * c_ref[CONV_W - 1:CONV_W, :]
    for d in range(1, CONV_W):
        acc = acc + pltpu.roll(xe, d, axis=0)[8:, :] * c_ref[CONV_W - 1 - d:CONV_W - d, :]
    return _silu(acc)


def _l2norm(y):
    return y * lax.rsqrt(jnp.sum(y * y, axis=-1, keepdims=True) + EPS)


def _group_cumsum_rows(x, group):
    lane = lax.broadcasted_iota(jnp.int32, x.shape, 1)
    pos = lane & (group - 1)
    s = 1
    while s < group:
        shifted = pltpu.roll(x, s, axis=1)
        x = x + jnp.where(pos >= s, shifted, 0.0)
        s *= 2
    return x


_B_NEG16, _B_C1, _B_C2, _B_EYE, _N_BMASKS = 0, 1, 2, 3, 4


def _col_replicated(row):
    return jnp.transpose(jnp.broadcast_to(row, (LANES, SB)))


def _gdn_prep(grp, q_ref, k_ref, v_ref, st_ref, alog_ref, dtb_ref, cq_ref, ck_ref, cv_ref,
              eq_ref, ek_ref, ev_ref, dst):
    lq_ref, kk_ref, rhs_ref, qe_ref, kdt_ref, gcc_ref, grs_ref = dst
    qa = _conv_silu(eq_ref, q_ref, cq_ref)
    ka = _conv_silu(ek_ref, k_ref, ck_ref)
    va = _conv_silu(ev_ref, v_ref, cv_ref)
    for j in range(HG):
        hh = grp * HG + j
        beta_row = jax.nn.sigmoid(st_ref[pl.ds(hh, 1), :])
        xg = st_ref[pl.ds(A_HEADS + hh, 1), :] + dtb_ref[pl.ds(hh, 1), :]
        softplus = jnp.maximum(xg, 0.0) + jnp.log1p(jnp.exp(-jnp.abs(xg)))
        g_row = -jnp.exp(alog_ref[pl.ds(hh, 1), :]) * softplus
        gc_row8 = _group_cumsum_rows(jnp.broadcast_to(g_row, (8, SB)), CHUNK)
        gc = _col_replicated(gc_row8[0:1])
        bc = _col_replicated(beta_row)
        grs_ref[j] = gc_row8
        gcc_ref[j] = gc

        q = _l2norm(qa[:, j * A_DK:(j + 1) * A_DK]) * (A_DK ** -0.5)
        k = _l2norm(ka[:, j * A_DK:(j + 1) * A_DK])
        v = va[:, j * A_DV:(j + 1) * A_DV]
        eg = jnp.exp(gc)
        kb = k * bc
        glast_rep = jnp.concatenate(
            [jnp.broadcast_to(gc[c * CHUNK + CHUNK - 1:(c + 1) * CHUNK, :], (CHUNK, LANES))
             for c in range(CPS)], axis=0)
        lq_ref[j] = jnp.concatenate([kb, q], axis=0).astype(BF16)
        kk_ref[j] = k.astype(BF16)
        rhs_ref[j] = jnp.concatenate([v * jnp.concatenate([bc, bc], axis=1), kb * eg],
                                     axis=1).astype(BF16)
        qe_ref[j] = (q * eg).astype(BF16)
        kdt_ref[j] = jnp.transpose(k * jnp.exp(glast_rep - gc)).astype(BF16)


def _gdn_main(fresh, src, z_ref, na_ref, o_ref, state_ref, causal_ref, bmask_ref, sel_ref):
    lq_ref, kk_ref, rhs_ref, qe_ref, kdt_ref, gcc_ref, grs_ref = src
    heads = range(HG)

    kq = [lax.dot_general(lq_ref[j], kk_ref[j], _NT, preferred_element_type=F32)
          for j in heads]

    def masked_scores(j):
        gc = gcc_ref[j]
        decay = jnp.exp(jnp.minimum(jnp.concatenate([gc, gc], axis=1) - grs_ref[j][0:1], 0.0)
                        ) * causal_ref[...]
        ab = (kq[j][:SB] * decay).astype(BF16)
        n1b = ab * bmask_ref[_B_NEG16]
        return ((kq[j][SB:] * decay).astype(BF16), n1b, bmask_ref[_B_EYE] + n1b,
                ab * bmask_ref[_B_C1], ab * bmask_ref[_B_C2])

    attn, n1b, pb, c1b, c2b = zip(*[masked_scores(j) for j in heads])

    n2b = [_dot(n1b[j], n1b[j]).astype(BF16) for j in heads]
    n4b = [_dot(n2b[j], n2b[j]).astype(BF16) for j in heads]
    n8b = [_dot(n4b[j], n4b[j]).astype(BF16) for j in heads]
    pb = [(pb[j].astype(F32) + _dot(pb[j], n2b[j])).astype(BF16) for j in heads]
    pb = [(pb[j].astype(F32) + _dot(pb[j], n4b[j])).astype(BF16) for j in heads]
    pb = [(pb[j].astype(F32) + _dot(pb[j], n8b[j])).astype(BF16) for j in heads]
    x1 = [_dot(pb[j], c1b[j]).astype(BF16) for j in heads]
    t32b = [(pb[j].astype(F32) - _dot(x1[j], pb[j])).astype(BF16) for j in heads]
    x2 = [_dot(t32b[j], c2b[j]).astype(BF16) for j in heads]
    tinv = [(t32b[j].astype(F32) - _dot(x2[j], t32b[j])).astype(BF16) for j in heads]

    uwb = [_dot(tinv[j], rhs_ref[j]).astype(BF16) for j in heads]
    aw = [_dot(attn[j], uwb[j]) for j in heads]
    qeff = [(qe_ref[j].astype(F32) - aw[j][:, A_DV:]).astype(BF16) for j in heads]
    bp = [_dot(jnp.concatenate([kdt_ref[j]] * CPS, axis=0) * sel_ref[...], uwb[j])
          for j in heads]

    st = [jnp.where(fresh, 0.0, state_ref[j]) for j in heads]
    outs = [[] for _ in heads]
    for c in range(CPS):
        for j in heads:
            blk = bp[j][c * A_DK:(c + 1) * A_DK]
            lhs = jnp.concatenate([blk[:, A_DV:].astype(BF16), qeff[j][c * CHUNK:(c + 1) * CHUNK]],
                                  axis=0)
            res = _dot(lhs, st[j].astype(BF16))
            outs[j].append(res[A_DK:])
            ld = jnp.exp(gcc_ref[j, c * CHUNK + CHUNK - 1:(c + 1) * CHUNK, :])
            st[j] = st[j] * jnp.concatenate([ld, ld], axis=1) + blk[:, :A_DV] - res[:A_DK]

    na = na_ref[...]
    for j in heads:
        state_ref[j] = st[j]
        o = jnp.concatenate(outs[j], axis=0) + aw[j][:, :A_DV]
        ms = jnp.mean(o * o, axis=-1, keepdims=True)
        z = z_ref[:, j * A_DV:(j + 1) * A_DV].astype(F32)
        o_ref[:, j * A_DV:(j + 1) * A_DV] = (
            o * lax.rsqrt(ms + EPS) * na * _silu(z)).astype(o_ref.dtype)


def _gdn_kernel(q_ref, k_ref, v_ref, z_ref, st_ref, alog_ref, dtb_ref, cq_ref, ck_ref, cv_ref,
                na_ref, o_ref, eq_ref, ek_ref, ev_ref, state_ref, causal_ref, bmask_ref, sel_ref, *stage):
    grp = pl.program_id(1)
    r = pl.program_id(2)
    stage_a, stage_b = stage[:len(stage) // 2], stage[len(stage) // 2:]

    @pl.when((pl.program_id(0) == 0) & (grp == 0) & (r == 0))
    def _():
        row = lax.broadcasted_iota(jnp.int32, (SB, SB), 0)
        col = lax.broadcasted_iota(jnp.int32, (SB, SB), 1)

        def same(bits):
            return ((row >> bits) == (col >> bits)).astype(F32)

        m16, m32, m64 = same(4), same(5), same(6)
        lower = (row >= col).astype(F32)
        causal_ref[...] = m64 * lower
        bmask_ref[_B_NEG16] = (-(m16 * (row > col).astype(F32))).astype(BF16)
        bmask_ref[_B_C1] = ((m32 - m16) * lower).astype(BF16)
        bmask_ref[_B_C2] = ((m64 - m32) * lower).astype(BF16)
        bmask_ref[_B_EYE] = (row == col).astype(F32).astype(BF16)
        rsel = lax.broadcasted_iota(jnp.int32, (CPS * A_DK, SB), 0)
        csel = lax.broadcasted_iota(jnp.int32, (CPS * A_DK, SB), 1)
        sel_ref[...] = ((rsel >> 7) == (csel >> 6)).astype(F32).astype(BF16)
        for ref in stage_b:
            ref[...] = jnp.zeros_like(ref)
        state_ref[...] = jnp.zeros_like(state_ref)

    @pl.when(r == 0)
    def _():
        eq_ref[...] = jnp.zeros_like(eq_ref)
        ek_ref[...] = jnp.zeros_like(ek_ref)
        ev_ref[...] = jnp.zeros_like(ev_ref)

    def step(dst, src):
        _gdn_prep(grp, q_ref, k_ref, v_ref, st_ref, alog_ref, dtb_ref, cq_ref, ck_ref, cv_ref,
                  eq_ref, ek_ref, ev_ref, dst)
        _gdn_main(r <= 1, src, z_ref, na_ref, o_ref, state_ref, causal_ref, bmask_ref, sel_ref)

    @pl.when(r % 2 == 0)
    def _():
        step(stage_a, stage_b)

    @pl.when(r % 2 == 1)
    def _():
        step(stage_b, stage_a)


def _gdn(proj, small_t, alog_b, dtb_b, conv_a, norm_a, batch, seq):
    t = proj.shape[0]
    nr = seq // SB
    wq, wv = HG * A_DK, HG * A_DV
    k_off = A_QK // wq
    v_off = (2 * A_QK) // wv
    z_off = (2 * A_QK + A_V) // wv

    def staged(b, g, r):
        return b * nr + jnp.minimum(r, nr - 1)

    def solved(b, g, r):
        return b * nr + jnp.maximum(r - 1, 0)

    stage = [
        pltpu.VMEM((HG, 2 * SB, A_DK), BF16),
        pltpu.VMEM((HG, SB, A_DK), BF16),
        pltpu.VMEM((HG, SB, A_DV + A_DK), BF16),
        pltpu.VMEM((HG, SB, A_DK), BF16),
        pltpu.VMEM((HG, A_DK, SB), BF16),
        pltpu.VMEM((HG, SB, LANES), F32),
        pltpu.VMEM((HG, 8, SB), F32),
    ]
    return pl.pallas_call(
        _gdn_kernel,
        grid=(batch, A_HEADS // HG, nr + 1),
        in_specs=[
            pl.BlockSpec((SB, wq), lambda b, g, r: (staged(b, g, r), g)),
            pl.BlockSpec((SB, wq), lambda b, g, r: (staged(b, g, r), k_off + g)),
            pl.BlockSpec((SB, wv), lambda b, g, r: (staged(b, g, r), v_off + g)),
            pl.BlockSpec((SB, wv), lambda b, g, r: (solved(b, g, r), z_off + g)),
            pl.BlockSpec((None, 2 * A_HEADS, SB), lambda b, g, r: (b, 0, jnp.minimum(r, nr - 1))),
            pl.BlockSpec((A_HEADS, SB), lambda b, g, r: (0, 0)),
            pl.BlockSpec((A_HEADS, SB), lambda b, g, r: (0, 0)),
            pl.BlockSpec((CONV_W, wq), lambda b, g, r: (0, g)),
            pl.BlockSpec((CONV_W, wq), lambda b, g, r: (0, k_off + g)),
            pl.BlockSpec((CONV_W, wv), lambda b, g, r: (0, v_off + g)),
            pl.BlockSpec((1, A_DV), lambda b, g, r: (0, 0)),
        ],
        out_specs=pl.BlockSpec((SB, wv), lambda b, g, r: (solved(b, g, r), g)),
        out_shape=jax.ShapeDtypeStruct((t, A_V), BF16),
        scratch_shapes=[
            pltpu.VMEM((8, wq), F32),
            pltpu.VMEM((8, wq), F32),
            pltpu.VMEM((8, wv), F32),
            pltpu.VMEM((HG, A_DK, A_DV), F32),
            pltpu.VMEM((SB, SB), F32),
            pltpu.VMEM((_N_BMASKS, SB, SB), BF16),
            pltpu.VMEM((CPS * A_DK, SB), BF16),
        ] + stage + stage,
        compiler_params=pltpu.CompilerParams(
            dimension_semantics=("arbitrary", "arbitrary", "arbitrary"),
            vmem_limit_bytes=VMEM_LIMIT),
        name="gdn",
    )(proj, proj, proj, proj, small_t, alog_b, dtb_b, conv_a, conv_a, conv_a, norm_a)


def _rotary(x, cos, sin):
    half = x.shape[-1] // 2
    x1, x2 = x[:, :half], x[:, half:]
    return jnp.concatenate([x1 * cos - x2 * sin, x1 * sin + x2 * cos], axis=1)


def _ret_kernel(q_ref, k_ref, v_ref, g_ref, cos_ref, sin_ref, nb_ref, o_ref, state_ref):
    h = pl.program_id(1)
    r = pl.program_id(2)
    lb = q_ref.shape[0]
    nsb = lb // SB

    @pl.when(r == 0)
    def _():
        state_ref[...] = jnp.zeros_like(state_ref)

    hf = jnp.full((SB, SB), h, jnp.int32).astype(F32)
    lg = jnp.log1p(-jnp.exp2(-5.0 - hf))
    row = lax.broadcasted_iota(jnp.int32, (SB, SB), 0)
    col = lax.broadcasted_iota(jnp.int32, (SB, SB), 1)
    allow = ((col >> 6) <= (row >> 6)).astype(F32)
    dist = jnp.abs(row - col).astype(F32)
    mask = jnp.exp(lg * dist) * allow
    rowf = row.astype(F32)
    q_dec = jnp.exp(lg * (rowf + 1.0))
    k_dec = jnp.exp(lg * (SB - 1.0 - rowf)) * (B_DK ** -0.5)
    sb_dec = jnp.exp(lg[0:1, 0:1] * SB)
    nb = nb_ref[...]

    state = state_ref[...]
    for s in range(nsb):
        r0 = s * SB
        cos = cos_ref[pl.ds(r0, SB), :]
        sin = sin_ref[pl.ds(r0, SB), :]
        q = _rotary(q_ref[pl.ds(r0, SB), :].astype(F32), cos, sin)
        k = _rotary(k_ref[pl.ds(r0, SB), :].astype(F32), cos, sin)
        vb = v_ref[pl.ds(r0, SB), :]
        sc = lax.dot_general(q.astype(BF16), (k * (B_DK ** -0.5)).astype(BF16), _NT,
                             preferred_element_type=F32)
        o = _dot((sc * mask).astype(BF16), vb) + _dot((q * q_dec).astype(BF16), state.astype(BF16))
        state = state * sb_dec + lax.dot_general(
            (k * k_dec).astype(BF16), vb, _TN, preferred_element_type=F32)
        ms = jnp.mean(o * o, axis=-1, keepdims=True)
        g = g_ref[pl.ds(r0, SB), :].astype(F32)
        o_ref[pl.ds(r0, SB), :] = (o * lax.rsqrt(ms + EPS) * nb * _silu(g)).astype(o_ref.dtype)
    state_ref[...] = state


def _ret(proj, cos, sin, norm_b, batch, seq, lb):
    t = proj.shape[0]
    nr = seq // lb
    base = 2 * A_QK + 2 * A_V
    q_off = base // B_DK
    k_off = (base + B_QK) // B_DK
    v_off = (base + 2 * B_QK) // B_DV
    g_off = (base + 2 * B_QK + B_V) // B_DV

    def rows(b, h, r):
        return b * nr + r

    return pl.pallas_call(
        _ret_kernel,
        grid=(batch, B_HEADS, nr),
        in_specs=[
            pl.BlockSpec((lb, B_DK), lambda b, h, r: (rows(b, h, r), q_off + h)),
            pl.BlockSpec((lb, B_DK), lambda b, h, r: (rows(b, h, r), k_off + h)),
            pl.BlockSpec((lb, B_DV), lambda b, h, r: (rows(b, h, r), v_off + h)),
            pl.BlockSpec((lb, B_DV), lambda b, h, r: (rows(b, h, r), g_off + h)),
            pl.BlockSpec((lb, B_DK // 2), lambda b, h, r: (r, 0)),
            pl.BlockSpec((lb, B_DK // 2), lambda b, h, r: (r, 0)),
            pl.BlockSpec((1, B_DV), lambda b, h, r: (0, h)),
        ],
        out_specs=pl.BlockSpec((lb, B_DV), lambda b, h, r: (rows(b, h, r), h)),
        out_shape=jax.ShapeDtypeStruct((t, B_V), BF16),
        scratch_shapes=[pltpu.VMEM((B_DK, B_DV), F32)],
        compiler_params=pltpu.CompilerParams(
            dimension_semantics=("arbitrary", "arbitrary", "arbitrary"),
            vmem_limit_bytes=VMEM_LIMIT),
        name="ret",
    )(proj, proj, proj, proj, cos, sin, norm_b)


def _merge_kernel(oa_ref, ob_ref, ga_ref, gb_ref, x_ref, wa_ref, wb_ref, wo_ref, n_ref, o_ref):
    ya = _dot(oa_ref[...], wa_ref[...])
    yb = _dot(ob_ref[...], wb_ref[...])
    ga = jax.nn.sigmoid(ga_ref[...].astype(F32))
    gb = jax.nn.sigmoid(gb_ref[...].astype(F32))
    mix = _dot((ga * ya + gb * yb).astype(BF16), wo_ref[...])
    ms = jnp.mean(mix * mix, axis=-1, keepdims=True)
    o_ref[...] = x_ref[...] + mix * lax.rsqrt(ms + EPS) * n_ref[...]


def _merge(oa, ob, proj, x2, wa, wb, wo, gain, tm):
    t, d = x2.shape
    gate_off = (2 * A_QK + 2 * A_V + 2 * B_QK + 2 * B_V) // d
    const = lambda i: (0, 0)
    return pl.pallas_call(
        _merge_kernel,
        grid=(t // tm,),
        in_specs=[
            pl.BlockSpec((tm, A_V), lambda i: (i, 0)),
            pl.BlockSpec((tm, B_V), lambda i: (i, 0)),
            pl.BlockSpec((tm, d), lambda i: (i, gate_off)),
            pl.BlockSpec((tm, d), lambda i: (i, gate_off + 1)),
            pl.BlockSpec((tm, d), lambda i: (i, 0)),
            pl.BlockSpec((A_V, d), const),
            pl.BlockSpec((B_V, d), const),
            pl.BlockSpec((d, d), const),
            pl.BlockSpec((1, d), const),
        ],
        out_specs=pl.BlockSpec((tm, d), lambda i: (i, 0)),
        out_shape=jax.ShapeDtypeStruct((t, d), F32),
        compiler_params=pltpu.CompilerParams(
            dimension_semantics=("arbitrary",), vmem_limit_bytes=VMEM_LIMIT),
        name="merge",
    )(oa, ob, proj, proj, x2, wa, wb, wo, gain)


def _mlp_kernel(x_ref, gpre_ref, wu_ref, wd_ref, gpost_ref, o_ref, *, ff_tile):
    x = x_ref[...]
    ms = jnp.mean(x * x, axis=-1, keepdims=True)
    hn = (x * lax.rsqrt(ms + EPS) * gpre_ref[...]).astype(BF16)
    d_ff = wu_ref.shape[1]
    ff = jnp.zeros(x.shape, F32)
    for j in range(d_ff // ff_tile):
        up = _dot(hn, wu_ref[:, j * ff_tile:(j + 1) * ff_tile])
        act = jnp.square(jnp.maximum(up, 0.0)).astype(BF16)
        ff = ff + _dot(act, wd_ref[j * ff_tile:(j + 1) * ff_tile, :])
    ms2 = jnp.mean(ff * ff, axis=-1, keepdims=True)
    o_ref[...] = x + ff * lax.rsqrt(ms2 + EPS) * gpost_ref[...]


def _mlp(x1, gpre, wu, wd, gpost, tm, ff_tile):
    t, d = x1.shape
    d_ff = wu.shape[1]
    const = lambda i: (0, 0)
    return pl.pallas_call(
        functools.partial(_mlp_kernel, ff_tile=ff_tile),
        grid=(t // tm,),
        in_specs=[
            pl.BlockSpec((tm, d), lambda i: (i, 0)),
            pl.BlockSpec((1, d), const),
            pl.BlockSpec((d, d_ff), const),
            pl.BlockSpec((d_ff, d), const),
            pl.BlockSpec((1, d), const),
        ],
        out_specs=pl.BlockSpec((tm, d), lambda i: (i, 0)),
        out_shape=jax.ShapeDtypeStruct((t, d), F32),
        compiler_params=pltpu.CompilerParams(
            dimension_semantics=("arbitrary",), vmem_limit_bytes=VMEM_LIMIT),
        name="mlp",
    )(x1, gpre, wu, wd, gpost)


def _pick(n, candidates):
    for c in candidates:
        if n % c == 0:
            return c
    raise ValueError(f"no tile for {n} among {candidates}")


def _layer(x, n_mix_pre, n_mix_post, n_mlp_pre, n_mlp_post, w_in, conv_a, a_log, dt_bias,
           norm_a, norm_b, w_br_a, w_br_b, w_out, w_up, w_down):
    batch, seq, d = x.shape
    t = batch * seq
    assert seq % SB == 0
    x2 = x.reshape(t, d)

    n_small_lo = 2 * A_QK + 2 * A_V
    n_small_hi = n_small_lo + 2 * A_HEADS
    w_main = jnp.concatenate([w_in[:, :n_small_lo], w_in[:, n_small_hi:]], axis=1).astype(BF16)
    w_small = jnp.pad(w_in[:, n_small_lo:n_small_hi], ((0, 0), (0, LANES - 2 * A_HEADS))).astype(BF16)

    tm = _pick(t, (1024, 512, 256))
    tn = _pick(w_main.shape[1], (2048, 1024, 512))
    proj, small = _inproj(x2, n_mix_pre.reshape(1, d), w_main, w_small, tm, tn)

    small_t = jnp.transpose(small[:, :2 * A_HEADS].reshape(batch, seq, 2 * A_HEADS), (0, 2, 1))
    alog_b = jnp.broadcast_to(a_log.reshape(A_HEADS, 1), (A_HEADS, SB))
    dtb_b = jnp.broadcast_to(dt_bias.reshape(A_HEADS, 1), (A_HEADS, SB))
    oa = _gdn(proj, small_t, alog_b, dtb_b, conv_a, norm_a.reshape(1, A_DV), batch, seq)

    lb = _pick(seq, (1024, 512, 256))
    inv_freq = ROPE_BASE ** (-jnp.arange(0, B_DK, 2, dtype=F32) / B_DK)
    ang = jnp.arange(seq, dtype=F32)[:, None] * inv_freq[None, :]
    ob = _ret(proj, jnp.cos(ang), jnp.sin(ang), norm_b.reshape(1, B_V), batch, seq, lb)

    tm2 = _pick(t, (512, 256))
    x1 = _merge(oa, ob, proj, x2, w_br_a.astype(BF16), w_br_b.astype(BF16), w_out.astype(BF16),
                n_mix_post.reshape(1, d), tm2)
    out = _mlp(x1, n_mlp_pre.reshape(1, d), w_up.astype(BF16), w_down.astype(BF16),
               n_mlp_post.reshape(1, d), tm2, 1024)
    return out.reshape(batch, seq, d)


def kernel(x, norm_mix_pre, norm_mix_post, norm_mlp_pre, norm_mlp_post, w_in, conv_a, a_log,
           dt_bias, norm_a, norm_b, w_br_a, w_br_b, w_out, w_up, w_down):
    for l in range(w_in.shape[0]):
        x = _layer(x, norm_mix_pre[l], norm_mix_post[l], norm_mlp_pre[l], norm_mlp_post[l],
                   w_in[l], conv_a[l], a_log[l], dt_bias[l], norm_a[l], norm_b[l],
                   w_br_a[l], w_br_b[l], w_out[l], w_up[l], w_down[l])
    return x
```

```python
import functools

import jax
import jax.numpy as jnp
from jax import lax
from jax.experimental import pallas as pl
from jax.experimental.pallas import tpu as pltpu

F32 = jnp.float32
BF16 = jnp.bfloat16

CHUNK = 64
EPS = 1e-6
A_HEADS, A_DK, A_DV, CONV_W = 8, 128, 256, 4
B_HEADS, B_DK, B_DV = 4, 256, 512
ROPE_BASE = 10000.0
A_QK, A_V = A_HEADS * A_DK, A_HEADS * A_DV
B_QK, B_V = B_HEADS * B_DK, B_HEADS * B_DV

SB = 256
CPS = SB // CHUNK
LANES = 128
HG = 4
VMEM_LIMIT = 56 * 1024 * 1024

_PW = 2048
_T_CONV_LAST, _T_Z, _T_ROT, _T_VB, _T_GB, _T_GATES = 1, 2, 3, 4, 5, 6

_NT = (((1,), (1,)), ((), ()))
_TN = (((0,), (0,)), ((), ()))


def _dot(a, b):
    return jnp.dot(a, b, preferred_element_type=F32)


def _silu(x):
    return x * jax.nn.sigmoid(x)


def _conv4(tail, acc, c_ref):
    xe = jnp.concatenate([tail, acc], axis=0)
    y = acc * c_ref[CONV_W - 1:CONV_W, :]
    for d in range(1, CONV_W):
        y = y + pltpu.roll(xe, d, axis=0)[8:, :] * c_ref[CONV_W - 1 - d:CONV_W - d, :]
    return y


def _inproj_kernel(x_ref, g_ref, w_ref, ws_ref, c_ref, cos_ref, sin_ref, o_ref, os_ref,
                   xn_ref, tail_ref, *, tiles_per_seq):
    i = pl.program_id(0)
    j = pl.program_id(1)
    tm = x_ref.shape[0]

    @pl.when(j == 0)
    def _():
        x = x_ref[...]
        ms = jnp.mean(x * x, axis=-1, keepdims=True)
        xn = (x * lax.rsqrt(ms + EPS) * g_ref[...]).astype(BF16)
        xn_ref[...] = xn
        os_ref[...] = _dot(xn, ws_ref[...])

    @pl.when((i == 0) & (j == 0))
    def _():
        tail_ref[...] = jnp.zeros_like(tail_ref)

    def proj():
        return _dot(xn_ref[...], w_ref[...])

    @pl.when(j <= _T_CONV_LAST)
    def _():
        acc = proj()
        tail = jnp.where(i % tiles_per_seq == 0, 0.0, tail_ref[j])
        tail_ref[j] = acc[tm - 8:, :]
        o_ref[...] = _silu(_conv4(tail, acc, c_ref)).astype(o_ref.dtype)

    @pl.when((j == _T_Z) | (j == _T_GB))
    def _():
        o_ref[...] = _silu(proj()).astype(o_ref.dtype)

    @pl.when(j == _T_ROT)
    def _():
        acc = proj()
        cos = cos_ref[...]
        sin = sin_ref[...]
        half = B_DK // 2
        for hd in range(_PW // B_DK):
            x1 = acc[:, hd * B_DK:hd * B_DK + half]
            x2 = acc[:, hd * B_DK + half:(hd + 1) * B_DK]
            o_ref[:, hd * B_DK:hd * B_DK + half] = (x1 * cos - x2 * sin).astype(o_ref.dtype)
            o_ref[:, hd * B_DK + half:(hd + 1) * B_DK] = (x1 * sin + x2 * cos).astype(o_ref.dtype)

    @pl.when((j == _T_VB) | (j == _T_GATES))
    def _():
        o_ref[...] = proj().astype(o_ref.dtype)


def _inproj(x2, gain, w_main, w_small, conv_a, cos, sin, tm, seq):
    t, d = x2.shape
    n = w_main.shape[1]
    assert n == 7 * _PW and seq % tm == 0
    tiles_per_seq = seq // tm
    return pl.pallas_call(
        functools.partial(_inproj_kernel, tiles_per_seq=tiles_per_seq),
        grid=(t // tm, n // _PW),
        in_specs=[
            pl.BlockSpec((tm, d), lambda i, j: (i, 0)),
            pl.BlockSpec((1, d), lambda i, j: (0, 0)),
            pl.BlockSpec((d, _PW), lambda i, j: (0, j)),
            pl.BlockSpec((d, LANES), lambda i, j: (0, 0)),
            pl.BlockSpec((CONV_W, _PW), lambda i, j: (0, jnp.minimum(j, _T_CONV_LAST))),
            pl.BlockSpec((tm, B_DK // 2), lambda i, j: (i % tiles_per_seq, 0)),
            pl.BlockSpec((tm, B_DK // 2), lambda i, j: (i % tiles_per_seq, 0)),
        ],
        out_specs=[
            pl.BlockSpec((tm, _PW), lambda i, j: (i, j)),
            pl.BlockSpec((tm, LANES), lambda i, j: (i, 0)),
        ],
        out_shape=[
            jax.ShapeDtypeStruct((t, n), BF16),
            jax.ShapeDtypeStruct((t, LANES), F32),
        ],
        scratch_shapes=[pltpu.VMEM((tm, d), BF16),
                        pltpu.VMEM((_T_CONV_LAST + 1, 8, _PW), F32)],
        compiler_params=pltpu.CompilerParams(
            dimension_semantics=("arbitrary", "arbitrary"), vmem_limit_bytes=VMEM_LIMIT),
        name="inproj",
    )(x2, gain, w_main, w_small, conv_a, cos, sin)


def _l2norm(y):
    return y * lax.rsqrt(jnp.sum(y * y, axis=-1, keepdims=True) + EPS)


def _group_cumsum_rows(x, group):
    lane = lax.broadcasted_iota(jnp.int32, x.shape, 1)
    pos = lane & (group - 1)
    s = 1
    while s < group:
        shifted = pltpu.roll(x, s, axis=1)
        x = x + jnp.where(pos >= s, shifted, 0.0)
        s *= 2
    return x


_B_NEG16, _B_C1, _B_C2, _B_EYE, _N_BMASKS = 0, 1, 2, 3, 4


def _col_replicated(row):
    return jnp.transpose(jnp.broadcast_to(row, (LANES, SB)))


def _gdn_prep(grp, q_ref, k_ref, v_ref, st_ref, alog_ref, dtb_ref, dst):
    lq_ref, kk_ref, rhs_ref, qe_ref, kdt_ref, gcc_ref, grs_ref = dst
    qa = q_ref[...].astype(F32)
    ka = k_ref[...].astype(F32)
    va = v_ref[...].astype(F32)
    for j in range(HG):
        hh = grp * HG + j
        beta_row = jax.nn.sigmoid(st_ref[pl.ds(hh, 1), :])
        xg = st_ref[pl.ds(A_HEADS + hh, 1), :] + dtb_ref[pl.ds(hh, 1), :]
        softplus = jnp.maximum(xg, 0.0) + jnp.log1p(jnp.exp(-jnp.abs(xg)))
        g_row = -jnp.exp(alog_ref[pl.ds(hh, 1), :]) * softplus
        gc_row8 = _group_cumsum_rows(jnp.broadcast_to(g_row, (8, SB)), CHUNK)
        gc = _col_replicated(gc_row8[0:1])
        bc = _col_replicated(beta_row)
        grs_ref[j] = gc_row8
        gcc_ref[j] = gc

        q = _l2norm(qa[:, j * A_DK:(j + 1) * A_DK]) * (A_DK ** -0.5)
        k = _l2norm(ka[:, j * A_DK:(j + 1) * A_DK])
        v = va[:, j * A_DV:(j + 1) * A_DV]
        eg = jnp.exp(gc)
        kb = k * bc
        glast_rep = jnp.concatenate(
            [jnp.broadcast_to(gc[c * CHUNK + CHUNK - 1:(c + 1) * CHUNK, :], (CHUNK, LANES))
             for c in range(CPS)], axis=0)
        lq_ref[j] = jnp.concatenate([kb, q], axis=0).astype(BF16)
        kk_ref[j] = k.astype(BF16)
        rhs_ref[j] = jnp.concatenate([v * jnp.concatenate([bc, bc], axis=1), kb * eg],
                                     axis=1).astype(BF16)
        qe_ref[j] = (q * eg).astype(BF16)
        kdt_ref[j] = jnp.transpose(k * jnp.exp(glast_rep - gc)).astype(BF16)


def _gdn_main(fresh, src, z_ref, na_ref, o_ref, state_ref, causal_ref, bmask_ref, sel_ref):
    lq_ref, kk_ref, rhs_ref, qe_ref, kdt_ref, gcc_ref, grs_ref = src
    heads = range(HG)

    kq = [lax.dot_general(lq_ref[j], kk_ref[j], _NT, preferred_element_type=F32)
          for j in heads]

    def masked_scores(j):
        gc = gcc_ref[j]
        decay = jnp.exp(jnp.minimum(jnp.concatenate([gc, gc], axis=1) - grs_ref[j][0:1], 0.0)
                        ) * causal_ref[...]
        ab = (kq[j][:SB] * decay).astype(BF16)
        n1b = ab * bmask_ref[_B_NEG16]
        return ((kq[j][SB:] * decay).astype(BF16), n1b, bmask_ref[_B_EYE] + n1b,
                ab * bmask_ref[_B_C1], ab * bmask_ref[_B_C2])

    attn, n1b, pb, c1b, c2b = zip(*[masked_scores(j) for j in heads])

    n2b = [_dot(n1b[j], n1b[j]).astype(BF16) for j in heads]
    n4b = [_dot(n2b[j], n2b[j]).astype(BF16) for j in heads]
    n8b = [_dot(n4b[j], n4b[j]).astype(BF16) for j in heads]
    pb = [(pb[j].astype(F32) + _dot(pb[j], n2b[j])).astype(BF16) for j in heads]
    pb = [(pb[j].astype(F32) + _dot(pb[j], n4b[j])).astype(BF16) for j in heads]
    pb = [(pb[j].astype(F32) + _dot(pb[j], n8b[j])).astype(BF16) for j in heads]
    x1 = [_dot(pb[j], c1b[j]).astype(BF16) for j in heads]
    t32b = [(pb[j].astype(F32) - _dot(x1[j], pb[j])).astype(BF16) for j in heads]
    x2 = [_dot(t32b[j], c2b[j]).astype(BF16) for j in heads]
    tinv = [(t32b[j].astype(F32) - _dot(x2[j], t32b[j])).astype(BF16) for j in heads]

    uwb = [_dot(tinv[j], rhs_ref[j]).astype(BF16) for j in heads]
    aw = [_dot(attn[j], uwb[j]) for j in heads]
    qeff = [(qe_ref[j].astype(F32) - aw[j][:, A_DV:]).astype(BF16) for j in heads]
    bp = [_dot(jnp.concatenate([kdt_ref[j]] * CPS, axis=0) * sel_ref[...], uwb[j])
          for j in heads]

    st = [jnp.where(fresh, 0.0, state_ref[j]) for j in heads]
    outs = [[] for _ in heads]
    for c in range(CPS):
        for j in heads:
            blk = bp[j][c * A_DK:(c + 1) * A_DK]
            lhs = jnp.concatenate([blk[:, A_DV:].astype(BF16), qeff[j][c * CHUNK:(c + 1) * CHUNK]],
                                  axis=0)
            res = _dot(lhs, st[j].astype(BF16))
            outs[j].append(res[A_DK:])
            ld = jnp.exp(gcc_ref[j, c * CHUNK + CHUNK - 1:(c + 1) * CHUNK, :])
            st[j] = st[j] * jnp.concatenate([ld, ld], axis=1) + blk[:, :A_DV] - res[:A_DK]

    na = na_ref[...]
    for j in heads:
        state_ref[j] = st[j]
        o = jnp.concatenate(outs[j], axis=0) + aw[j][:, :A_DV]
        ms = jnp.mean(o * o, axis=-1, keepdims=True)
        gate = z_ref[:, j * A_DV:(j + 1) * A_DV].astype(F32)
        o_ref[:, j * A_DV:(j + 1) * A_DV] = (
            o * lax.rsqrt(ms + EPS) * na * gate).astype(o_ref.dtype)


def _gdn_kernel(q_ref, k_ref, v_ref, z_ref, st_ref, alog_ref, dtb_ref, na_ref, o_ref,
                state_ref, causal_ref, bmask_ref, sel_ref, *stage):
    grp = pl.program_id(1)
    r = pl.program_id(2)
    stage_a, stage_b = stage[:len(stage) // 2], stage[len(stage) // 2:]

    @pl.when((pl.program_id(0) == 0) & (grp == 0) & (r == 0))
    def _():
        row = lax.broadcasted_iota(jnp.int32, (SB, SB), 0)
        col = lax.broadcasted_iota(jnp.int32, (SB, SB), 1)

        def same(bits):
            return ((row >> bits) == (col >> bits)).astype(F32)

        m16, m32, m64 = same(4), same(5), same(6)
        lower = (row >= col).astype(F32)
        causal_ref[...] = m64 * lower
        bmask_ref[_B_NEG16] = (-(m16 * (row > col).astype(F32))).astype(BF16)
        bmask_ref[_B_C1] = ((m32 - m16) * lower).astype(BF16)
        bmask_ref[_B_C2] = ((m64 - m32) * lower).astype(BF16)
        bmask_ref[_B_EYE] = (row == col).astype(F32).astype(BF16)
        rsel = lax.broadcasted_iota(jnp.int32, (CPS * A_DK, SB), 0)
        csel = lax.broadcasted_iota(jnp.int32, (CPS * A_DK, SB), 1)
        sel_ref[...] = ((rsel >> 7) == (csel >> 6)).astype(F32).astype(BF16)
        for ref in stage_b:
            ref[...] = jnp.zeros_like(ref)
        state_ref[...] = jnp.zeros_like(state_ref)

    def step(dst, src):
        _gdn_prep(grp, q_ref, k_ref, v_ref, st_ref, alog_ref, dtb_ref, dst)
        _gdn_main(r <= 1, src, z_ref, na_ref, o_ref, state_ref, causal_ref, bmask_ref, sel_ref)

    @pl.when(r % 2 == 0)
    def _():
        step(stage_a, stage_b)

    @pl.when(r % 2 == 1)
    def _():
        step(stage_b, stage_a)


def _gdn(proj, small_t, alog_b, dtb_b, norm_a, batch, seq):
    t = proj.shape[0]
    nr = seq // SB
    wq, wv = HG * A_DK, HG * A_DV
    k_off = A_QK // wq
    v_off = (2 * A_QK) // wv
    z_off = (2 * A_QK + A_V) // wv

    def staged(b, g, r):
        return b * nr + jnp.minimum(r, nr - 1)

    def solved(b, g, r):
        return b * nr + jnp.maximum(r - 1, 0)

    stage = [
        pltpu.VMEM((HG, 2 * SB, A_DK), BF16),
        pltpu.VMEM((HG, SB, A_DK), BF16),
        pltpu.VMEM((HG, SB, A_DV + A_DK), BF16),
        pltpu.VMEM((HG, SB, A_DK), BF16),
        pltpu.VMEM((HG, A_DK, SB), BF16),
        pltpu.VMEM((HG, SB, LANES), F32),
        pltpu.VMEM((HG, 8, SB), F32),
    ]
    return pl.pallas_call(
        _gdn_kernel,
        grid=(batch, A_HEADS // HG, nr + 1),
        in_specs=[
            pl.BlockSpec((SB, wq), lambda b, g, r: (staged(b, g, r), g)),
            pl.BlockSpec((SB, wq), lambda b, g, r: (staged(b, g, r), k_off + g)),
            pl.BlockSpec((SB, wv), lambda b, g, r: (staged(b, g, r), v_off + g)),
            pl.BlockSpec((SB, wv), lambda b, g, r: (solved(b, g, r), z_off + g)),
            pl.BlockSpec((None, 2 * A_HEADS, SB), lambda b, g, r: (b, 0, jnp.minimum(r, nr - 1))),
            pl.BlockSpec((A_HEADS, SB), lambda b, g, r: (0, 0)),
            pl.BlockSpec((A_HEADS, SB), lambda b, g, r: (0, 0)),
            pl.BlockSpec((1, A_DV), lambda b, g, r: (0, 0)),
        ],
        out_specs=pl.BlockSpec((SB, wv), lambda b, g, r: (solved(b, g, r), g)),
        out_shape=jax.ShapeDtypeStruct((t, A_V), BF16),
        scratch_shapes=[
            pltpu.VMEM((HG, A_DK, A_DV), F32),
            pltpu.VMEM((SB, SB), F32),
            pltpu.VMEM((_N_BMASKS, SB, SB), BF16),
            pltpu.VMEM((CPS * A_DK, SB), BF16),
        ] + stage + stage,
        compiler_params=pltpu.CompilerParams(
            dimension_semantics=("arbitrary", "arbitrary", "arbitrary"),
            vmem_limit_bytes=VMEM_LIMIT),
        name="gdn",
    )(proj, proj, proj, proj, small_t, alog_b, dtb_b, norm_a)


def _ret_kernel(q_ref, k_ref, v_ref, g_ref, nb_ref, o_ref, state_ref):
    h = pl.program_id(1)
    r = pl.program_id(2)
    lb = q_ref.shape[0]
    nsb = lb // SB

    @pl.when(r == 0)
    def _():
        state_ref[...] = jnp.zeros_like(state_ref)

    hf = jnp.full((SB, SB), h, jnp.int32).astype(F32)
    lg = jnp.log1p(-jnp.exp2(-5.0 - hf))
    row = lax.broadcasted_iota(jnp.int32, (SB, SB), 0)
    col = lax.broadcasted_iota(jnp.int32, (SB, SB), 1)
    allow = ((col >> 6) <= (row >> 6)).astype(F32)
    dist = jnp.abs(row - col).astype(F32)
    mask = jnp.exp(lg * dist) * allow
    rowf = row.astype(F32)
    q_dec = jnp.exp(lg * (rowf + 1.0))
    k_dec = jnp.exp(lg * (SB - 1.0 - rowf)) * (B_DK ** -0.5)
    sb_dec = jnp.exp(lg[0:1, 0:1] * SB)
    nb = nb_ref[...]

    state = state_ref[...]
    for s in range(nsb):
        r0 = s * SB
        qb = q_ref[pl.ds(r0, SB), :]
        q = qb.astype(F32)
        k = k_ref[pl.ds(r0, SB), :].astype(F32)
        vb = v_ref[pl.ds(r0, SB), :]
        sc = lax.dot_general(qb, (k * (B_DK ** -0.5)).astype(BF16), _NT,
                             preferred_element_type=F32)
        o = _dot((sc * mask).astype(BF16), vb) + _dot((q * q_dec).astype(BF16), state.astype(BF16))
        state = state * sb_dec + lax.dot_general(
            (k * k_dec).astype(BF16), vb, _TN, preferred_element_type=F32)
        ms = jnp.mean(o * o, axis=-1, keepdims=True)
        gate = g_ref[pl.ds(r0, SB), :].astype(F32)
        o_ref[pl.ds(r0, SB), :] = (o * lax.rsqrt(ms + EPS) * nb * gate).astype(o_ref.dtype)
    state_ref[...] = state


def _ret(proj, norm_b, batch, seq, lb):
    t = proj.shape[0]
    nr = seq // lb
    base = 2 * A_QK + 2 * A_V
    q_off = base // B_DK
    k_off = (base + B_QK) // B_DK
    v_off = (base + 2 * B_QK) // B_DV
    g_off = (base + 2 * B_QK + B_V) // B_DV

    def rows(b, h, r):
        return b * nr + r

    return pl.pallas_call(
        _ret_kernel,
        grid=(batch, B_HEADS, nr),
        in_specs=[
            pl.BlockSpec((lb, B_DK), lambda b, h, r: (rows(b, h, r), q_off + h)),
            pl.BlockSpec((lb, B_DK), lambda b, h, r: (rows(b, h, r), k_off + h)),
            pl.BlockSpec((lb, B_DV), lambda b, h, r: (rows(b, h, r), v_off + h)),
            pl.BlockSpec((lb, B_DV), lambda b, h, r: (rows(b, h, r), g_off + h)),
            pl.BlockSpec((1, B_DV), lambda b, h, r: (0, h)),
        ],
        out_specs=pl.BlockSpec((lb, B_DV), lambda b, h, r: (rows(b, h, r), h)),
        out_shape=jax.ShapeDtypeStruct((t, B_V), BF16),
        scratch_shapes=[pltpu.VMEM((B_DK, B_DV), F32)],
        compiler_params=pltpu.CompilerParams(
            dimension_semantics=("arbitrary", "arbitrary", "arbitrary"),
            vmem_limit_bytes=VMEM_LIMIT),
        name="ret",
    )(proj, proj, proj, proj, norm_b)


def _merge_kernel(oa_ref, ob_ref, ga_ref, gb_ref, x_ref, wa_ref, wb_ref, wo_ref, n_ref, o_ref):
    ya = _dot(oa_ref[...], wa_ref[...])
    yb = _dot(ob_ref[...], wb_ref[...])
    ga = jax.nn.sigmoid(ga_ref[...].astype(F32))
    gb = jax.nn.sigmoid(gb_ref[...].astype(F32))
    mix = _dot((ga * ya + gb * yb).astype(BF16), wo_ref[...])
    ms = jnp.mean(mix * mix, axis=-1, keepdims=True)
    o_ref[...] = x_ref[...] + mix * lax.rsqrt(ms + EPS) * n_ref[...]


def _merge(oa, ob, proj, x2, wa, wb, wo, gain, tm):
    t, d = x2.shape
    gate_off = (2 * A_QK + 2 * A_V + 2 * B_QK + 2 * B_V) // d
    const = lambda i: (0, 0)
    return pl.pallas_call(
        _merge_kernel,
        grid=(t // tm,),
        in_specs=[
            pl.BlockSpec((tm, A_V), lambda i: (i, 0)),
            pl.BlockSpec((tm, B_V), lambda i: (i, 0)),
            pl.BlockSpec((tm, d), lambda i: (i, gate_off)),
            pl.BlockSpec((tm, d), lambda i: (i, gate_off + 1)),
            pl.BlockSpec((tm, d), lambda i: (i, 0)),
            pl.BlockSpec((A_V, d), const),
            pl.BlockSpec((B_V, d), const),
            pl.BlockSpec((d, d), const),
            pl.BlockSpec((1, d), const),
        ],
        out_specs=pl.BlockSpec((tm, d), lambda i: (i, 0)),
        out_shape=jax.ShapeDtypeStruct((t, d), F32),
        compiler_params=pltpu.CompilerParams(
            dimension_semantics=("arbitrary",), vmem_limit_bytes=VMEM_LIMIT),
        name="merge",
    )(oa, ob, proj, proj, x2, wa, wb, wo, gain)


def _mlp_kernel(x_ref, gpre_ref, wu_ref, wd_ref, gpost_ref, o_ref, *, ff_tile):
    x = x_ref[...]
    ms = jnp.mean(x * x, axis=-1, keepdims=True)
    hn = (x * lax.rsqrt(ms + EPS) * gpre_ref[...]).astype(BF16)
    d_ff = wu_ref.shape[1]
    ff = jnp.zeros(x.shape, F32)
    for j in range(d_ff // ff_tile):
        up = _dot(hn, wu_ref[:, j * ff_tile:(j + 1) * ff_tile])
        act = jnp.square(jnp.maximum(up, 0.0)).astype(BF16)
        ff = ff + _dot(act, wd_ref[j * ff_tile:(j + 1) * ff_tile, :])
    ms2 = jnp.mean(ff * ff, axis=-1, keepdims=True)
    o_ref[...] = x + ff * lax.rsqrt(ms2 + EPS) * gpost_ref[...]


def _mlp(x1, gpre, wu, wd, gpost, tm, ff_tile):
    t, d = x1.shape
    d_ff = wu.shape[1]
    const = lambda i: (0, 0)
    return pl.pallas_call(
        functools.partial(_mlp_kernel, ff_tile=ff_tile),
        grid=(t // tm,),
        in_specs=[
            pl.BlockSpec((tm, d), lambda i: (i, 0)),
            pl.BlockSpec((1, d), const),
            pl.BlockSpec((d, d_ff), const),
            pl.BlockSpec((d_ff, d), const),
            pl.BlockSpec((1, d), const),
        ],
        out_specs=pl.BlockSpec((tm, d), lambda i: (i, 0)),
        out_shape=jax.ShapeDtypeStruct((t, d), F32),
        compiler_params=pltpu.CompilerParams(
            dimension_semantics=("arbitrary",), vmem_limit_bytes=VMEM_LIMIT),
        name="mlp",
    )(x1, gpre, wu, wd, gpost)


def _pick(n, candidates):
    for c in candidates:
        if n % c == 0:
            return c
    raise ValueError(f"no tile for {n} among {candidates}")


def _layer(x, n_mix_pre, n_mix_post, n_mlp_pre, n_mlp_post, w_in, conv_a, a_log, dt_bias,
           norm_a, norm_b, w_br_a, w_br_b, w_out, w_up, w_down):
    batch, seq, d = x.shape
    t = batch * seq
    assert seq % SB == 0
    x2 = x.reshape(t, d)

    n_small_lo = 2 * A_QK + 2 * A_V
    n_small_hi = n_small_lo + 2 * A_HEADS
    w16 = w_in.astype(BF16)
    w_main = jnp.concatenate([w16[:, :n_small_lo], w16[:, n_small_hi:]], axis=1)
    w_small = jnp.pad(w16[:, n_small_lo:n_small_hi], ((0, 0), (0, LANES - 2 * A_HEADS)))

    inv_freq = ROPE_BASE ** (-jnp.arange(0, B_DK, 2, dtype=F32) / B_DK)
    ang = jnp.arange(seq, dtype=F32)[:, None] * inv_freq[None, :]

    tm = _pick(seq, (1024, 512, 256))
    proj, small = _inproj(x2, n_mix_pre.reshape(1, d), w_main, w_small, conv_a,
                          jnp.cos(ang), jnp.sin(ang), tm, seq)

    small_t = jnp.transpose(small[:, :2 * A_HEADS].reshape(batch, seq, 2 * A_HEADS), (0, 2, 1))
    alog_b = jnp.broadcast_to(a_log.reshape(A_HEADS, 1), (A_HEADS, SB))
    dtb_b = jnp.broadcast_to(dt_bias.reshape(A_HEADS, 1), (A_HEADS, SB))
    oa = _gdn(proj, small_t, alog_b, dtb_b, norm_a.reshape(1, A_DV), batch, seq)

    lb = _pick(seq, (1024, 512, 256))
    ob = _ret(proj, norm_b.reshape(1, B_V), batch, seq, lb)

    tm2 = _pick(t, (512, 256))
    x1 = _merge(oa, ob, proj, x2, w_br_a.astype(BF16), w_br_b.astype(BF16), w_out.astype(BF16),
                n_mix_post.reshape(1, d), tm2)
    out = _mlp(x1, n_mlp_pre.reshape(1, d), w_up.astype(BF16), w_down.astype(BF16),
               n_mlp_post.reshape(1, d), tm2, 1024)
    return out.reshape(batch, seq, d)


def kernel(x, norm_mix_pre, norm_mix_post, norm_mlp_pre, norm_mlp_post, w_in, conv_a, a_log,
           dt_bias, norm_a, norm_b, w_br_a, w_br_b, w_out, w_up, w_down):
    for l in range(w_in.shape[0]):
        x = _layer(x, norm_mix_pre[l], norm_mix_post[l], norm_mlp_pre[l], norm_mlp_post[l],
                   w_in[l], conv_a[l], a_log[l], dt_bias[l], norm_a[l], norm_b[l],
                   w_br_a[l], w_br_b[l], w_out[l], w_up[l], w_down[l])
    return x
```

```python
import functools

import jax
import jax.numpy as jnp
from jax import lax
from jax.experimental import pallas as pl
from jax.experimental.pallas import tpu as pltpu

F32 = jnp.float32
BF16 = jnp.bfloat16

CHUNK = 64
EPS = 1e-6
A_HEADS, A_DK, A_DV, CONV_W = 8, 128, 256, 4
B_HEADS, B_DK, B_DV = 4, 256, 512
ROPE_BASE = 10000.0
A_QK, A_V = A_HEADS * A_DK, A_HEADS * A_DV
B_QK, B_V = B_HEADS * B_DK, B_HEADS * B_DV

SB = 256
CPS = SB // CHUNK
LANES = 128
HG = 8
VMEM_LIMIT = 56 * 1024 * 1024

_PW = 2048
_T_ROT = (2 * A_QK + 2 * A_V) // _PW
_T_GB = (2 * A_QK + 2 * A_V + 2 * B_QK + B_V) // _PW

_NT = (((1,), (1,)), ((), ()))
_TN = (((0,), (0,)), ((), ()))


def _dot(a, b):
    return jnp.dot(a, b, preferred_element_type=F32)


def _silu(x):
    return x * jax.nn.sigmoid(x)


def _inproj_kernel(x_ref, g_ref, w_ref, ws_ref, cos_ref, sin_ref, o_ref, os_ref, xn_ref):
    j = pl.program_id(1)

    @pl.when(j == 0)
    def _():
        x = x_ref[...]
        ms = jnp.mean(x * x, axis=-1, keepdims=True)
        xn = (x * lax.rsqrt(ms + EPS) * g_ref[...]).astype(BF16)
        xn_ref[...] = xn
        os_ref[...] = _dot(xn, ws_ref[...])

    def proj():
        return _dot(xn_ref[...], w_ref[...])

    @pl.when(j == _T_ROT)
    def _():
        acc = proj()
        cos = cos_ref[...]
        sin = sin_ref[...]
        half = B_DK // 2
        for hd in range(_PW // B_DK):
            x1 = acc[:, hd * B_DK:hd * B_DK + half]
            x2 = acc[:, hd * B_DK + half:(hd + 1) * B_DK]
            o_ref[:, hd * B_DK:hd * B_DK + half] = (x1 * cos - x2 * sin).astype(o_ref.dtype)
            o_ref[:, hd * B_DK + half:(hd + 1) * B_DK] = (x1 * sin + x2 * cos).astype(o_ref.dtype)

    @pl.when(j == _T_GB)
    def _():
        o_ref[...] = _silu(proj()).astype(o_ref.dtype)

    @pl.when((j != _T_ROT) & (j != _T_GB))
    def _():
        o_ref[...] = proj().astype(o_ref.dtype)


def _inproj(x2, gain, w_main, w_small, cos, sin, tm, seq):
    t, d = x2.shape
    n = w_main.shape[1]
    assert n % _PW == 0 and seq % tm == 0
    tiles_per_seq = seq // tm
    return pl.pallas_call(
        _inproj_kernel,
        grid=(t // tm, n // _PW),
        in_specs=[
            pl.BlockSpec((tm, d), lambda i, j: (i, 0)),
            pl.BlockSpec((1, d), lambda i, j: (0, 0)),
            pl.BlockSpec((d, _PW), lambda i, j: (0, j)),
            pl.BlockSpec((d, LANES), lambda i, j: (0, 0)),
            pl.BlockSpec((tm, B_DK // 2), lambda i, j: (i % tiles_per_seq, 0)),
            pl.BlockSpec((tm, B_DK // 2), lambda i, j: (i % tiles_per_seq, 0)),
        ],
        out_specs=[
            pl.BlockSpec((tm, _PW), lambda i, j: (i, j)),
            pl.BlockSpec((tm, LANES), lambda i, j: (i, 0)),
        ],
        out_shape=[
            jax.ShapeDtypeStruct((t, n), BF16),
            jax.ShapeDtypeStruct((t, LANES), F32),
        ],
        scratch_shapes=[pltpu.VMEM((tm, d), BF16)],
        compiler_params=pltpu.CompilerParams(
            dimension_semantics=("arbitrary", "arbitrary"), vmem_limit_bytes=VMEM_LIMIT),
        name="inproj",
    )(x2, gain, w_main, w_small, cos, sin)


def _conv_silu(tail_ref, x_ref, c_ref):
    x = x_ref[...].astype(F32)
    rows = x.shape[0]
    xe = jnp.concatenate([tail_ref[...], x], axis=0)
    tail_ref[...] = x[rows - 8:, :]
    acc = x * c_ref[CONV_W - 1:CONV_W, :]
    for d in range(1, CONV_W):
        acc = acc + pltpu.roll(xe, d, axis=0)[8:, :] * c_ref[CONV_W - 1 - d:CONV_W - d, :]
    return _silu(acc)


def _l2norm(y):
    return y * lax.rsqrt(jnp.sum(y * y, axis=-1, keepdims=True) + EPS)


def _group_cumsum_rows(x, group):
    lane = lax.broadcasted_iota(jnp.int32, x.shape, 1)
    pos = lane & (group - 1)
    s = 1
    while s < group:
        shifted = pltpu.roll(x, s, axis=1)
        x = x + jnp.where(pos >= s, shifted, 0.0)
        s *= 2
    return x


_B_NEG16, _B_C1, _B_C2, _B_EYE, _B_CAUSAL, _N_BMASKS = 0, 1, 2, 3, 4, 5


def _col_replicated(row):
    return jnp.transpose(jnp.broadcast_to(row, (LANES, SB)))


def _gdn_prep(grp, q_ref, k_ref, v_ref, st_ref, alog_ref, dtb_ref, cq_ref, ck_ref, cv_ref,
              eq_ref, ek_ref, ev_ref, dst):
    lq_ref, kk_ref, rhs_ref, qe_ref, kdt_ref, gcc_ref, grs_ref = dst
    qa = _conv_silu(eq_ref, q_ref, cq_ref)
    ka = _conv_silu(ek_ref, k_ref, ck_ref)
    va = _conv_silu(ev_ref, v_ref, cv_ref)
    for j in range(HG):
        hh = grp * HG + j
        beta_row = jax.nn.sigmoid(st_ref[pl.ds(hh, 1), :])
        xg = st_ref[pl.ds(A_HEADS + hh, 1), :] + dtb_ref[pl.ds(hh, 1), :]
        softplus = jnp.maximum(xg, 0.0) + jnp.log1p(jnp.exp(-jnp.abs(xg)))
        g_row = -jnp.exp(alog_ref[pl.ds(hh, 1), :]) * softplus
        gc_row8 = _group_cumsum_rows(jnp.broadcast_to(g_row, (8, SB)), CHUNK)
        gc = _col_replicated(gc_row8[0:1])
        bc = _col_replicated(beta_row)
        grs_ref[j] = gc_row8
        gcc_ref[j] = gc

        q = _l2norm(qa[:, j * A_DK:(j + 1) * A_DK]) * (A_DK ** -0.5)
        k = _l2norm(ka[:, j * A_DK:(j + 1) * A_DK])
        v = va[:, j * A_DV:(j + 1) * A_DV]
        eg = jnp.exp(gc)
        kb = k * bc
        glast_rep = jnp.concatenate(
            [jnp.broadcast_to(gc[c * CHUNK + CHUNK - 1:(c + 1) * CHUNK, :], (CHUNK, LANES))
             for c in range(CPS)], axis=0)
        lq_ref[j] = jnp.concatenate([kb, q], axis=0).astype(BF16)
        kk_ref[j] = k.astype(BF16)
        rhs_ref[j] = jnp.concatenate([v * jnp.concatenate([bc, bc], axis=1), kb * eg],
                                     axis=1).astype(BF16)
        qe_ref[j] = (q * eg).astype(BF16)
        kdt_ref[j] = jnp.transpose(k * jnp.exp(glast_rep - gc)).astype(BF16)


def _gdn_main(fresh, src, z_ref, na_ref, o_ref, state_ref, bmask_ref, sel_ref):
    lq_ref, kk_ref, rhs_ref, qe_ref, kdt_ref, gcc_ref, grs_ref = src
    heads = range(HG)

    kq = [lax.dot_general(lq_ref[j], kk_ref[j], _NT, preferred_element_type=F32)
          for j in heads]

    def masked_scores(j):
        gc = gcc_ref[j]
        decay = jnp.exp(jnp.minimum(jnp.concatenate([gc, gc], axis=1) - grs_ref[j][0:1], 0.0))
        ab = (kq[j][:SB] * decay).astype(BF16)
        n1b = ab * bmask_ref[_B_NEG16]
        return ((kq[j][SB:] * decay).astype(BF16) * bmask_ref[_B_CAUSAL], n1b,
                bmask_ref[_B_EYE] + n1b, ab * bmask_ref[_B_C1], ab * bmask_ref[_B_C2])

    attn, n1b, pb, c1b, c2b = zip(*[masked_scores(j) for j in heads])
    eye = bmask_ref[_B_EYE]

    n2b = [_dot(n1b[j], n1b[j]).astype(BF16) for j in heads]
    pb = [_dot(pb[j], eye + n2b[j]).astype(BF16) for j in heads]
    n4b = [_dot(n2b[j], n2b[j]).astype(BF16) for j in heads]
    pb = [_dot(pb[j], eye + n4b[j]).astype(BF16) for j in heads]
    n8b = [_dot(n4b[j], n4b[j]).astype(BF16) for j in heads]
    pb = [_dot(pb[j], eye + n8b[j]).astype(BF16) for j in heads]
    x1 = [eye - _dot(pb[j], c1b[j]).astype(BF16) for j in heads]
    t32b = [_dot(x1[j], pb[j]).astype(BF16) for j in heads]
    x2 = [eye - _dot(t32b[j], c2b[j]).astype(BF16) for j in heads]
    tinv = [_dot(x2[j], t32b[j]).astype(BF16) for j in heads]

    uwb = [_dot(tinv[j], rhs_ref[j]).astype(BF16) for j in heads]
    aw = [_dot(attn[j], uwb[j]) for j in heads]
    qeff = [(qe_ref[j].astype(F32) - aw[j][:, A_DV:]).astype(BF16) for j in heads]
    bp = [_dot(jnp.concatenate([kdt_ref[j]] * CPS, axis=0) * sel_ref[...], uwb[j])
          for j in heads]

    st = [jnp.where(fresh, 0.0, state_ref[j]) for j in heads]
    outs = [[] for _ in heads]
    for c in range(CPS):
        for j in heads:
            blk = bp[j][c * A_DK:(c + 1) * A_DK]
            lhs = jnp.concatenate([blk[:, A_DV:].astype(BF16), qeff[j][c * CHUNK:(c + 1) * CHUNK]],
                                  axis=0)
            res = _dot(lhs, st[j].astype(BF16))
            outs[j].append(res[A_DK:])
            ld = jnp.exp(gcc_ref[j, c * CHUNK + CHUNK - 1:(c + 1) * CHUNK, :])
            st[j] = st[j] * jnp.concatenate([ld, ld], axis=1) + blk[:, :A_DV] - res[:A_DK]

    na = na_ref[...]
    for j in heads:
        state_ref[j] = st[j]
        o = jnp.concatenate(outs[j], axis=0) + aw[j][:, :A_DV]
        ms = jnp.mean(o * o, axis=-1, keepdims=True)
        z = z_ref[:, j * A_DV:(j + 1) * A_DV].astype(F32)
        o_ref[:, j * A_DV:(j + 1) * A_DV] = (
            o * lax.rsqrt(ms + EPS) * na * _silu(z)).astype(o_ref.dtype)


def _gdn_kernel(q_ref, k_ref, v_ref, z_ref, st_ref, alog_ref, dtb_ref, cq_ref, ck_ref, cv_ref,
                na_ref, o_ref, eq_ref, ek_ref, ev_ref, state_ref, bmask_ref, sel_ref, *stage):
    grp = pl.program_id(1)
    r = pl.program_id(2)
    stage_a, stage_b = stage[:len(stage) // 2], stage[len(stage) // 2:]

    @pl.when((pl.program_id(0) == 0) & (grp == 0) & (r == 0))
    def _():
        row = lax.broadcasted_iota(jnp.int32, (SB, SB), 0)
        col = lax.broadcasted_iota(jnp.int32, (SB, SB), 1)

        def same(bits):
            return ((row >> bits) == (col >> bits)).astype(F32)

        m16, m32, m64 = same(4), same(5), same(6)
        lower = (row >= col).astype(F32)
        bmask_ref[_B_CAUSAL] = (m64 * lower).astype(BF16)
        bmask_ref[_B_NEG16] = (-(m16 * (row > col).astype(F32))).astype(BF16)
        bmask_ref[_B_C1] = ((m32 - m16) * lower).astype(BF16)
        bmask_ref[_B_C2] = ((m64 - m32) * lower).astype(BF16)
        bmask_ref[_B_EYE] = (row == col).astype(F32).astype(BF16)
        rsel = lax.broadcasted_iota(jnp.int32, (CPS * A_DK, SB), 0)
        csel = lax.broadcasted_iota(jnp.int32, (CPS * A_DK, SB), 1)
        sel_ref[...] = ((rsel >> 7) == (csel >> 6)).astype(F32).astype(BF16)
        for ref in stage_b:
            ref[...] = jnp.zeros_like(ref)
        state_ref[...] = jnp.zeros_like(state_ref)

    @pl.when(r == 0)
    def _():
        eq_ref[...] = jnp.zeros_like(eq_ref)
        ek_ref[...] = jnp.zeros_like(ek_ref)
        ev_ref[...] = jnp.zeros_like(ev_ref)

    def step(dst, src):
        _gdn_prep(grp, q_ref, k_ref, v_ref, st_ref, alog_ref, dtb_ref, cq_ref, ck_ref, cv_ref,
                  eq_ref, ek_ref, ev_ref, dst)
        _gdn_main(r <= 1, src, z_ref, na_ref, o_ref, state_ref, bmask_ref, sel_ref)

    @pl.when(r % 2 == 0)
    def _():
        step(stage_a, stage_b)

    @pl.when(r % 2 == 1)
    def _():
        step(stage_b, stage_a)


def _gdn(proj, small_t, alog_b, dtb_b, conv_a, norm_a, batch, seq):
    t = proj.shape[0]
    nr = seq // SB
    wq, wv = HG * A_DK, HG * A_DV
    k_off = A_QK // wq
    v_off = (2 * A_QK) // wv
    z_off = (2 * A_QK + A_V) // wv

    def staged(b, g, r):
        return b * nr + jnp.minimum(r, nr - 1)

    def solved(b, g, r):
        return b * nr + jnp.maximum(r - 1, 0)

    stage = [
        pltpu.VMEM((HG, 2 * SB, A_DK), BF16),
        pltpu.VMEM((HG, SB, A_DK), BF16),
        pltpu.VMEM((HG, SB, A_DV + A_DK), BF16),
        pltpu.VMEM((HG, SB, A_DK), BF16),
        pltpu.VMEM((HG, A_DK, SB), BF16),
        pltpu.VMEM((HG, SB, LANES), F32),
        pltpu.VMEM((HG, 8, SB), F32),
    ]
    return pl.pallas_call(
        _gdn_kernel,
        grid=(batch, A_HEADS // HG, nr + 1),
        in_specs=[
            pl.BlockSpec((SB, wq), lambda b, g, r: (staged(b, g, r), g)),
            pl.BlockSpec((SB, wq), lambda b, g, r: (staged(b, g, r), k_off + g)),
            pl.BlockSpec((SB, wv), lambda b, g, r: (staged(b, g, r), v_off + g)),
            pl.BlockSpec((SB, wv), lambda b, g, r: (solved(b, g, r), z_off + g)),
            pl.BlockSpec((None, 2 * A_HEADS, SB), lambda b, g, r: (b, 0, jnp.minimum(r, nr - 1))),
            pl.BlockSpec((A_HEADS, SB), lambda b, g, r: (0, 0)),
            pl.BlockSpec((A_HEADS, SB), lambda b, g, r: (0, 0)),
            pl.BlockSpec((CONV_W, wq), lambda b, g, r: (0, g)),
            pl.BlockSpec((CONV_W, wq), lambda b, g, r: (0, k_off + g)),
            pl.BlockSpec((CONV_W, wv), lambda b, g, r: (0, v_off + g)),
            pl.BlockSpec((1, A_DV), lambda b, g, r: (0, 0)),
        ],
        out_specs=pl.BlockSpec((SB, wv), lambda b, g, r: (solved(b, g, r), g)),
        out_shape=jax.ShapeDtypeStruct((t, A_V), BF16),
        scratch_shapes=[
            pltpu.VMEM((8, wq), F32),
            pltpu.VMEM((8, wq), F32),
            pltpu.VMEM((8, wv), F32),
            pltpu.VMEM((HG, A_DK, A_DV), F32),
            pltpu.VMEM((_N_BMASKS, SB, SB), BF16),
            pltpu.VMEM((CPS * A_DK, SB), BF16),
        ] + stage + stage,
        compiler_params=pltpu.CompilerParams(
            dimension_semantics=("arbitrary", "arbitrary", "arbitrary"),
            vmem_limit_bytes=VMEM_LIMIT),
        name="gdn",
    )(proj, proj, proj, proj, small_t, alog_b, dtb_b, conv_a, conv_a, conv_a, norm_a)


_D_MASK, _D_Q, _D_K, _D_SB, _N_DEC = 0, 1, 2, 3, 4


def _ret_kernel(q_ref, k_ref, v_ref, g_ref, nb_ref, o_ref, state_ref, dec_ref):
    h = pl.program_id(1)
    r = pl.program_id(2)
    lb = q_ref.shape[0]
    nsb = lb // SB

    @pl.when(r == 0)
    def _():
        state_ref[...] = jnp.zeros_like(state_ref)
        hf = jnp.full((SB, SB), h, jnp.int32).astype(F32)
        lg = jnp.log1p(-jnp.exp2(-5.0 - hf))
        row = lax.broadcasted_iota(jnp.int32, (SB, SB), 0)
        col = lax.broadcasted_iota(jnp.int32, (SB, SB), 1)
        allow = ((col >> 6) <= (row >> 6)).astype(F32)
        dist = jnp.abs(row - col).astype(F32)
        rowf = row.astype(F32)
        dec_ref[_D_MASK] = jnp.exp(lg * dist) * allow * (B_DK ** -0.5)
        dec_ref[_D_Q] = jnp.exp(lg * (rowf + 1.0))
        dec_ref[_D_K] = jnp.exp(lg * (SB - 1.0 - rowf)) * (B_DK ** -0.5)
        dec_ref[_D_SB] = jnp.exp(lg * SB)

    mask = dec_ref[_D_MASK]
    q_dec = dec_ref[_D_Q]
    k_dec = dec_ref[_D_K]
    sb_dec = dec_ref[_D_SB][0:1, 0:1]
    nb = nb_ref[...]

    state = state_ref[...]
    for s in range(nsb):
        r0 = s * SB
        qb = q_ref[pl.ds(r0, SB), :]
        kb = k_ref[pl.ds(r0, SB), :]
        vb = v_ref[pl.ds(r0, SB), :]
        sc = lax.dot_general(qb, kb, _NT, preferred_element_type=F32)
        o = (_dot((sc * mask).astype(BF16), vb)
             + _dot((qb.astype(F32) * q_dec).astype(BF16), state.astype(BF16)))
        state = state * sb_dec + lax.dot_general(
            (kb.astype(F32) * k_dec).astype(BF16), vb, _TN, preferred_element_type=F32)
        ms = jnp.mean(o * o, axis=-1, keepdims=True)
        gate = g_ref[pl.ds(r0, SB), :].astype(F32)
        o_ref[pl.ds(r0, SB), :] = (o * lax.rsqrt(ms + EPS) * nb * gate).astype(o_ref.dtype)
    state_ref[...] = state


def _ret(proj, norm_b, batch, seq, lb):
    t = proj.shape[0]
    nr = seq // lb
    base = 2 * A_QK + 2 * A_V
    q_off = base // B_DK
    k_off = (base + B_QK) // B_DK
    v_off = (base + 2 * B_QK) // B_DV
    g_off = (base + 2 * B_QK + B_V) // B_DV

    def rows(b, h, r):
        return b * nr + r

    return pl.pallas_call(
        _ret_kernel,
        grid=(batch, B_HEADS, nr),
        in_specs=[
            pl.BlockSpec((lb, B_DK), lambda b, h, r: (rows(b, h, r), q_off + h)),
            pl.BlockSpec((lb, B_DK), lambda b, h, r: (rows(b, h, r), k_off + h)),
            pl.BlockSpec((lb, B_DV), lambda b, h, r: (rows(b, h, r), v_off + h)),
            pl.BlockSpec((lb, B_DV), lambda b, h, r: (rows(b, h, r), g_off + h)),
            pl.BlockSpec((1, B_DV), lambda b, h, r: (0, h)),
        ],
        out_specs=pl.BlockSpec((lb, B_DV), lambda b, h, r: (rows(b, h, r), h)),
        out_shape=jax.ShapeDtypeStruct((t, B_V), BF16),
        scratch_shapes=[pltpu.VMEM((B_DK, B_DV), F32), pltpu.VMEM((_N_DEC, SB, SB), F32)],
        compiler_params=pltpu.CompilerParams(
            dimension_semantics=("arbitrary", "arbitrary", "arbitrary"),
            vmem_limit_bytes=VMEM_LIMIT),
        name="ret",
    )(proj, proj, proj, proj, norm_b)


def _merge_kernel(oa_ref, ob_ref, ga_ref, gb_ref, x_ref, wa_ref, wb_ref, wo_ref, n_ref, o_ref):
    ya = _dot(oa_ref[...], wa_ref[...])
    yb = _dot(ob_ref[...], wb_ref[...])
    ga = jax.nn.sigmoid(ga_ref[...].astype(F32))
    gb = jax.nn.sigmoid(gb_ref[...].astype(F32))
    mix = _dot((ga * ya + gb * yb).astype(BF16), wo_ref[...])
    ms = jnp.mean(mix * mix, axis=-1, keepdims=True)
    o_ref[...] = x_ref[...] + mix * lax.rsqrt(ms + EPS) * n_ref[...]


def _merge(oa, ob, proj, x2, wa, wb, wo, gain, tm):
    t, d = x2.shape
    gate_off = (2 * A_QK + 2 * A_V + 2 * B_QK + 2 * B_V) // d
    const = lambda i: (0, 0)
    return pl.pallas_call(
        _merge_kernel,
        grid=(t // tm,),
        in_specs=[
            pl.BlockSpec((tm, A_V), lambda i: (i, 0)),
            pl.BlockSpec((tm, B_V), lambda i: (i, 0)),
            pl.BlockSpec((tm, d), lambda i: (i, gate_off)),
            pl.BlockSpec((tm, d), lambda i: (i, gate_off + 1)),
            pl.BlockSpec((tm, d), lambda i: (i, 0)),
            pl.BlockSpec((A_V, d), const),
            pl.BlockSpec((B_V, d), const),
            pl.BlockSpec((d, d), const),
            pl.BlockSpec((1, d), const),
        ],
        out_specs=pl.BlockSpec((tm, d), lambda i: (i, 0)),
        out_shape=jax.ShapeDtypeStruct((t, d), F32),
        compiler_params=pltpu.CompilerParams(
            dimension_semantics=("arbitrary",), vmem_limit_bytes=VMEM_LIMIT),
        name="merge",
    )(oa, ob, proj, proj, x2, wa, wb, wo, gain)


def _mlp_kernel(x_ref, gpre_ref, wu_ref, wd_ref, gpost_ref, o_ref, *, ff_tile):
    x = x_ref[...]
    ms = jnp.mean(x * x, axis=-1, keepdims=True)
    hn = (x * lax.rsqrt(ms + EPS) * gpre_ref[...]).astype(BF16)
    d_ff = wu_ref.shape[1]
    ff = jnp.zeros(x.shape, F32)
    for j in range(d_ff // ff_tile):
        up = _dot(hn, wu_ref[:, j * ff_tile:(j + 1) * ff_tile])
        act = jnp.square(jnp.maximum(up, 0.0)).astype(BF16)
        ff = ff + _dot(act, wd_ref[j * ff_tile:(j + 1) * ff_tile, :])
    ms2 = jnp.mean(ff * ff, axis=-1, keepdims=True)
    o_ref[...] = x + ff * lax.rsqrt(ms2 + EPS) * gpost_ref[...]


def _mlp(x1, gpre, wu, wd, gpost, tm, ff_tile):
    t, d = x1.shape
    d_ff = wu.shape[1]
    const = lambda i: (0, 0)
    return pl.pallas_call(
        functools.partial(_mlp_kernel, ff_tile=ff_tile),
        grid=(t // tm,),
        in_specs=[
            pl.BlockSpec((tm, d), lambda i: (i, 0)),
            pl.BlockSpec((1, d), const),
            pl.BlockSpec((d, d_ff), const),
            pl.BlockSpec((d_ff, d), const),
            pl.BlockSpec((1, d), const),
        ],
        out_specs=pl.BlockSpec((tm, d), lambda i: (i, 0)),
        out_shape=jax.ShapeDtypeStruct((t, d), F32),
        compiler_params=pltpu.CompilerParams(
            dimension_semantics=("arbitrary",), vmem_limit_bytes=VMEM_LIMIT),
        name="mlp",
    )(x1, gpre, wu, wd, gpost)


def _pick(n, candidates):
    for c in candidates:
        if n % c == 0:
            return c
    raise ValueError(f"no tile for {n} among {candidates}")


def _layer(x, n_mix_pre, n_mix_post, n_mlp_pre, n_mlp_post, w_in, conv_a, a_log, dt_bias,
           norm_a, norm_b, w_br_a, w_br_b, w_out, w_up, w_down):
    batch, seq, d = x.shape
    t = batch * seq
    assert seq % SB == 0
    x2 = x.reshape(t, d)

    n_small_lo = 2 * A_QK + 2 * A_V
    n_small_hi = n_small_lo + 2 * A_HEADS
    w_main = jnp.concatenate([w_in[:, :n_small_lo], w_in[:, n_small_hi:]], axis=1).astype(BF16)
    w_small = jnp.pad(w_in[:, n_small_lo:n_small_hi], ((0, 0), (0, LANES - 2 * A_HEADS))).astype(BF16)

    inv_freq = ROPE_BASE ** (-jnp.arange(0, B_DK, 2, dtype=F32) / B_DK)
    ang = jnp.arange(seq, dtype=F32)[:, None] * inv_freq[None, :]

    tm = _pick(seq, (1024, 512, 256))
    proj, small = _inproj(x2, n_mix_pre.reshape(1, d), w_main, w_small,
                          jnp.cos(ang), jnp.sin(ang), tm, seq)

    small_t = jnp.transpose(small[:, :2 * A_HEADS].reshape(batch, seq, 2 * A_HEADS), (0, 2, 1))
    alog_b = jnp.broadcast_to(a_log.reshape(A_HEADS, 1), (A_HEADS, SB))
    dtb_b = jnp.broadcast_to(dt_bias.reshape(A_HEADS, 1), (A_HEADS, SB))
    oa = _gdn(proj, small_t, alog_b, dtb_b, conv_a, norm_a.reshape(1, A_DV), batch, seq)

    lb = _pick(seq, (1024, 512, 256))
    ob = _ret(proj, norm_b.reshape(1, B_V), batch, seq, lb)

    tm2 = _pick(t, (512, 256))
    x1 = _merge(oa, ob, proj, x2, w_br_a.astype(BF16), w_br_b.astype(BF16), w_out.astype(BF16),
                n_mix_post.reshape(1, d), tm2)
    out = _mlp(x1, n_mlp_pre.reshape(1, d), w_up.astype(BF16), w_down.astype(BF16),
               n_mlp_post.reshape(1, d), tm2, 1024)
    return out.reshape(batch, seq, d)


def kernel(x, norm_mix_pre, norm_mix_post, norm_mlp_pre, norm_mlp_post, w_in, conv_a, a_log,
           dt_bias, norm_a, norm_b, w_br_a, w_br_b, w_out, w_up, w_down):
    for l in range(w_in.shape[0]):
        x = _layer(x, norm_mix_pre[l], norm_mix_post[l], norm_mlp_pre[l], norm_mlp_post[l],
                   w_in[l], conv_a[l], a_log[l], dt_bias[l], norm_a[l], norm_b[l],
                   w_br_a[l], w_br_b[l], w_out[l], w_up[l], w_down[l])
    return x
```

```python
import functools

import jax
import jax.numpy as jnp
from jax import lax
from jax.experimental import pallas as pl
from jax.experimental.pallas import tpu as pltpu

F32 = jnp.float32
BF16 = jnp.bfloat16

CHUNK = 64
EPS = 1e-6
A_HEADS, A_DK, A_DV, CONV_W = 8, 128, 256, 4
B_HEADS, B_DK, B_DV = 4, 256, 512
ROPE_BASE = 10000.0
A_QK, A_V = A_HEADS * A_DK, A_HEADS * A_DV
B_QK, B_V = B_HEADS * B_DK, B_HEADS * B_DV

SB = 256
CPS = SB // CHUNK
LANES = 128
HG = 8
VMEM_LIMIT = 56 * 1024 * 1024

_PW = 2048
_T_ROT = (2 * A_QK + 2 * A_V) // _PW
_T_GB = (2 * A_QK + 2 * A_V + 2 * B_QK + B_V) // _PW

_NT = (((1,), (1,)), ((), ()))
_TN = (((0,), (0,)), ((), ()))


def _dot(a, b):
    return jnp.dot(a, b, preferred_element_type=F32)


def _silu(x):
    h = 0.5 * x
    return h * (1.0 + jnp.tanh(h))


def _inproj_kernel(x_ref, g_ref, w_ref, ws_ref, cos_ref, sin_ref, o_ref, os_ref, xn_ref):
    j = pl.program_id(1)

    @pl.when(j == 0)
    def _():
        x = x_ref[...]
        ms = jnp.mean(x * x, axis=-1, keepdims=True)
        xn = (x * lax.rsqrt(ms + EPS) * g_ref[...]).astype(BF16)
        xn_ref[...] = xn
        os_ref[...] = _dot(xn, ws_ref[...])

    def proj():
        return _dot(xn_ref[...], w_ref[...])

    @pl.when(j == _T_ROT)
    def _():
        acc = proj()
        cos = cos_ref[...]
        sin = sin_ref[...]
        half = B_DK // 2
        for hd in range(_PW // B_DK):
            x1 = acc[:, hd * B_DK:hd * B_DK + half]
            x2 = acc[:, hd * B_DK + half:(hd + 1) * B_DK]
            o_ref[:, hd * B_DK:hd * B_DK + half] = (x1 * cos - x2 * sin).astype(o_ref.dtype)
            o_ref[:, hd * B_DK + half:(hd + 1) * B_DK] = (x1 * sin + x2 * cos).astype(o_ref.dtype)

    @pl.when(j == _T_GB)
    def _():
        o_ref[...] = _silu(proj()).astype(o_ref.dtype)

    @pl.when((j != _T_ROT) & (j != _T_GB))
    def _():
        o_ref[...] = proj().astype(o_ref.dtype)


def _inproj(x2, gain, w_main, w_small, cos, sin, tm, seq):
    t, d = x2.shape
    n = w_main.shape[1]
    assert n % _PW == 0 and seq % tm == 0
    tiles_per_seq = seq // tm
    return pl.pallas_call(
        _inproj_kernel,
        grid=(t // tm, n // _PW),
        in_specs=[
            pl.BlockSpec((tm, d), lambda i, j: (i, 0)),
            pl.BlockSpec((1, d), lambda i, j: (0, 0)),
            pl.BlockSpec((d, _PW), lambda i, j: (0, j)),
            pl.BlockSpec((d, LANES), lambda i, j: (0, 0)),
            pl.BlockSpec((tm, B_DK // 2), lambda i, j: (i % tiles_per_seq, 0)),
            pl.BlockSpec((tm, B_DK // 2), lambda i, j: (i % tiles_per_seq, 0)),
        ],
        out_specs=[
            pl.BlockSpec((tm, _PW), lambda i, j: (i, j)),
            pl.BlockSpec((tm, LANES), lambda i, j: (i, 0)),
        ],
        out_shape=[
            jax.ShapeDtypeStruct((t, n), BF16),
            jax.ShapeDtypeStruct((t, LANES), F32),
        ],
        scratch_shapes=[pltpu.VMEM((tm, d), BF16)],
        compiler_params=pltpu.CompilerParams(
            dimension_semantics=("arbitrary", "arbitrary"), vmem_limit_bytes=VMEM_LIMIT),
        name="inproj",
    )(x2, gain, w_main, w_small, cos, sin)


def _conv_silu(tail_ref, x_ref, c_ref):
    x = x_ref[...].astype(F32)
    rows = x.shape[0]
    xe = jnp.concatenate([tail_ref[...], x], axis=0)
    tail_ref[...] = x[rows - 8:, :]
    c_half = 0.5 * c_ref[...]
    h = x * c_half[CONV_W - 1:CONV_W, :]
    for d in range(1, CONV_W):
        h = h + pltpu.roll(xe, d, axis=0)[8:, :] * c_half[CONV_W - 1 - d:CONV_W - d, :]
    return h * (1.0 + jnp.tanh(h))


def _l2norm(y):
    return y * lax.rsqrt(jnp.sum(y * y, axis=-1, keepdims=True) + EPS)


def _group_cumsum_rows(x, group):
    lane = lax.broadcasted_iota(jnp.int32, x.shape, 1)
    pos = lane & (group - 1)
    s = 1
    while s < group:
        shifted = pltpu.roll(x, s, axis=1)
        x = x + jnp.where(pos >= s, shifted, 0.0)
        s *= 2
    return x


_B_NEG16, _B_C1, _B_C2, _B_EYE, _B_CAUSAL, _N_BMASKS = 0, 1, 2, 3, 4, 5


def _col_replicated(row):
    return jnp.transpose(jnp.broadcast_to(row, (LANES, SB)))


def _gdn_prep(grp, q_ref, k_ref, v_ref, st_ref, alog_ref, dtb_ref, cq_ref, ck_ref, cv_ref,
              eq_ref, ek_ref, ev_ref, dst):
    lq_ref, kk_ref, rhs_ref, qe_ref, kdt_ref, gcc_ref, grs_ref = dst
    qa = _conv_silu(eq_ref, q_ref, cq_ref)
    ka = _conv_silu(ek_ref, k_ref, ck_ref)
    va = _conv_silu(ev_ref, v_ref, cv_ref)
    for j in range(HG):
        hh = grp * HG + j
        beta_row = jax.nn.sigmoid(st_ref[pl.ds(hh, 1), :])
        xg = st_ref[pl.ds(A_HEADS + hh, 1), :] + dtb_ref[pl.ds(hh, 1), :]
        softplus = jnp.maximum(xg, 0.0) + jnp.log1p(jnp.exp(-jnp.abs(xg)))
        g_row = -jnp.exp(alog_ref[pl.ds(hh, 1), :]) * softplus
        gc_row8 = _group_cumsum_rows(jnp.broadcast_to(g_row, (8, SB)), CHUNK)
        gc = _col_replicated(gc_row8[0:1])
        bc = _col_replicated(beta_row)
        grs_ref[j] = gc_row8
        gcc_ref[j] = gc

        q = _l2norm(qa[:, j * A_DK:(j + 1) * A_DK]) * (A_DK ** -0.5)
        k = _l2norm(ka[:, j * A_DK:(j + 1) * A_DK])
        v = va[:, j * A_DV:(j + 1) * A_DV]
        eg = jnp.exp(gc)
        kb = k * bc
        glast_rep = jnp.concatenate(
            [jnp.broadcast_to(gc[c * CHUNK + CHUNK - 1:(c + 1) * CHUNK, :], (CHUNK, LANES))
             for c in range(CPS)], axis=0)
        lq_ref[j] = jnp.concatenate([kb, q], axis=0).astype(BF16)
        kk_ref[j] = k.astype(BF16)
        rhs_ref[j] = jnp.concatenate([v * jnp.concatenate([bc, bc], axis=1), kb * eg],
                                     axis=1).astype(BF16)
        qe_ref[j] = (q * eg).astype(BF16)
        kdt_ref[j] = jnp.transpose(k * jnp.exp(glast_rep - gc)).astype(BF16)


def _gdn_main(fresh, src, z_ref, na_ref, o_ref, state_ref, bmask_ref, sel_ref):
    lq_ref, kk_ref, rhs_ref, qe_ref, kdt_ref, gcc_ref, grs_ref = src
    heads = range(HG)

    kq = [lax.dot_general(lq_ref[j], kk_ref[j], _NT, preferred_element_type=F32)
          for j in heads]

    def masked_scores(j):
        gc = gcc_ref[j]
        decay = jnp.exp(jnp.minimum(jnp.concatenate([gc, gc], axis=1) - grs_ref[j][0:1], 0.0))
        ab = (kq[j][:SB] * decay).astype(BF16)
        n1b = ab * bmask_ref[_B_NEG16]
        return ((kq[j][SB:] * decay).astype(BF16) * bmask_ref[_B_CAUSAL], n1b,
                bmask_ref[_B_EYE] + n1b, ab * bmask_ref[_B_C1], ab * bmask_ref[_B_C2])

    attn, n1b, pb, c1b, c2b = zip(*[masked_scores(j) for j in heads])
    eye = bmask_ref[_B_EYE]

    n2b = [_dot(n1b[j], n1b[j]).astype(BF16) for j in heads]
    pb = [_dot(pb[j], eye + n2b[j]).astype(BF16) for j in heads]
    n4b = [_dot(n2b[j], n2b[j]).astype(BF16) for j in heads]
    pb = [_dot(pb[j], eye + n4b[j]).astype(BF16) for j in heads]
    n8b = [_dot(n4b[j], n4b[j]).astype(BF16) for j in heads]
    pb = [_dot(pb[j], eye + n8b[j]).astype(BF16) for j in heads]
    tinv = pb
    for blk, cb in ((16, c1b), (32, c2b)):
        low = lambda x: jnp.concatenate(
            [x[b + blk:b + 2 * blk] for b in range(0, SB, 2 * blk)], axis=0)
        eye_low = low(eye)
        xm = [eye_low - _dot(low(tinv[j]), cb[j]).astype(BF16) for j in heads]
        t_low = [_dot(xm[j], tinv[j]).astype(BF16) for j in heads]
        tinv = [jnp.concatenate(
            [piece for i, b in enumerate(range(0, SB, 2 * blk))
             for piece in (tinv[j][b:b + blk], t_low[j][i * blk:(i + 1) * blk])], axis=0)
            for j in heads]

    uwb = [_dot(tinv[j], rhs_ref[j]).astype(BF16) for j in heads]
    aw = [_dot(attn[j], uwb[j]) for j in heads]
    qeff = [(qe_ref[j].astype(F32) - aw[j][:, A_DV:]).astype(BF16) for j in heads]
    bp = [_dot(jnp.concatenate([kdt_ref[j]] * CPS, axis=0) * sel_ref[...], uwb[j])
          for j in heads]

    st = [jnp.where(fresh, 0.0, state_ref[j]) for j in heads]
    outs = [[] for _ in heads]
    for c in range(CPS):
        for j in heads:
            blk = bp[j][c * A_DK:(c + 1) * A_DK]
            lhs = jnp.concatenate([blk[:, A_DV:].astype(BF16), qeff[j][c * CHUNK:(c + 1) * CHUNK]],
                                  axis=0)
            res = _dot(lhs, st[j].astype(BF16))
            outs[j].append(res[A_DK:])
            ld = jnp.exp(gcc_ref[j, c * CHUNK + CHUNK - 1:(c + 1) * CHUNK, :])
            st[j] = st[j] * jnp.concatenate([ld, ld], axis=1) + blk[:, :A_DV] - res[:A_DK]

    na = na_ref[...]
    for j in heads:
        state_ref[j] = st[j]
        o = jnp.concatenate(outs[j], axis=0) + aw[j][:, :A_DV]
        ms = jnp.mean(o * o, axis=-1, keepdims=True)
        z = z_ref[:, j * A_DV:(j + 1) * A_DV].astype(F32)
        o_ref[:, j * A_DV:(j + 1) * A_DV] = (
            o * lax.rsqrt(ms + EPS) * na * _silu(z)).astype(o_ref.dtype)


def _gdn_kernel(q_ref, k_ref, v_ref, z_ref, st_ref, alog_ref, dtb_ref, cq_ref, ck_ref, cv_ref,
                na_ref, o_ref, eq_ref, ek_ref, ev_ref, state_ref, bmask_ref, sel_ref, *stage):
    grp = pl.program_id(1)
    r = pl.program_id(2)
    stage_a, stage_b = stage[:len(stage) // 2], stage[len(stage) // 2:]

    @pl.when((pl.program_id(0) == 0) & (grp == 0) & (r == 0))
    def _():
        row = lax.broadcasted_iota(jnp.int32, (SB, SB), 0)
        col = lax.broadcasted_iota(jnp.int32, (SB, SB), 1)

        def same(bits):
            return ((row >> bits) == (col >> bits)).astype(F32)

        m16, m32, m64 = same(4), same(5), same(6)
        lower = (row >= col).astype(F32)
        bmask_ref[_B_CAUSAL] = (m64 * lower).astype(BF16)
        bmask_ref[_B_NEG16] = (-(m16 * (row > col).astype(F32))).astype(BF16)
        bmask_ref[_B_C1] = ((m32 - m16) * lower).astype(BF16)
        bmask_ref[_B_C2] = ((m64 - m32) * lower).astype(BF16)
        bmask_ref[_B_EYE] = (row == col).astype(F32).astype(BF16)
        rsel = lax.broadcasted_iota(jnp.int32, (CPS * A_DK, SB), 0)
        csel = lax.broadcasted_iota(jnp.int32, (CPS * A_DK, SB), 1)
        sel_ref[...] = ((rsel >> 7) == (csel >> 6)).astype(F32).astype(BF16)
        for ref in stage_b:
            ref[...] = jnp.zeros_like(ref)
        state_ref[...] = jnp.zeros_like(state_ref)

    @pl.when(r == 0)
    def _():
        eq_ref[...] = jnp.zeros_like(eq_ref)
        ek_ref[...] = jnp.zeros_like(ek_ref)
        ev_ref[...] = jnp.zeros_like(ev_ref)

    def step(dst, src):
        _gdn_prep(grp, q_ref, k_ref, v_ref, st_ref, alog_ref, dtb_ref, cq_ref, ck_ref, cv_ref,
                  eq_ref, ek_ref, ev_ref, dst)
        _gdn_main(r <= 1, src, z_ref, na_ref, o_ref, state_ref, bmask_ref, sel_ref)

    @pl.when(r % 2 == 0)
    def _():
        step(stage_a, stage_b)

    @pl.when(r % 2 == 1)
    def _():
        step(stage_b, stage_a)


def _gdn(proj, small_t, alog_b, dtb_b, conv_a, norm_a, batch, seq):
    t = proj.shape[0]
    nr = seq // SB
    wq, wv = HG * A_DK, HG * A_DV
    k_off = A_QK // wq
    v_off = (2 * A_QK) // wv
    z_off = (2 * A_QK + A_V) // wv

    def staged(b, g, r):
        return b * nr + jnp.minimum(r, nr - 1)

    def solved(b, g, r):
        return b * nr + jnp.maximum(r - 1, 0)

    stage = [
        pltpu.VMEM((HG, 2 * SB, A_DK), BF16),
        pltpu.VMEM((HG, SB, A_DK), BF16),
        pltpu.VMEM((HG, SB, A_DV + A_DK), BF16),
        pltpu.VMEM((HG, SB, A_DK), BF16),
        pltpu.VMEM((HG, A_DK, SB), BF16),
        pltpu.VMEM((HG, SB, LANES), F32),
        pltpu.VMEM((HG, 8, SB), F32),
    ]
    return pl.pallas_call(
        _gdn_kernel,
        grid=(batch, A_HEADS // HG, nr + 1),
        in_specs=[
            pl.BlockSpec((SB, wq), lambda b, g, r: (staged(b, g, r), g)),
            pl.BlockSpec((SB, wq), lambda b, g, r: (staged(b, g, r), k_off + g)),
            pl.BlockSpec((SB, wv), lambda b, g, r: (staged(b, g, r), v_off + g)),
            pl.BlockSpec((SB, wv), lambda b, g, r: (solved(b, g, r), z_off + g)),
            pl.BlockSpec((None, 2 * A_HEADS, SB), lambda b, g, r: (b, 0, jnp.minimum(r, nr - 1))),
            pl.BlockSpec((A_HEADS, SB), lambda b, g, r: (0, 0)),
            pl.BlockSpec((A_HEADS, SB), lambda b, g, r: (0, 0)),
            pl.BlockSpec((CONV_W, wq), lambda b, g, r: (0, g)),
            pl.BlockSpec((CONV_W, wq), lambda b, g, r: (0, k_off + g)),
            pl.BlockSpec((CONV_W, wv), lambda b, g, r: (0, v_off + g)),
            pl.BlockSpec((1, A_DV), lambda b, g, r: (0, 0)),
        ],
        out_specs=pl.BlockSpec((SB, wv), lambda b, g, r: (solved(b, g, r), g)),
        out_shape=jax.ShapeDtypeStruct((t, A_V), BF16),
        scratch_shapes=[
            pltpu.VMEM((8, wq), F32),
            pltpu.VMEM((8, wq), F32),
            pltpu.VMEM((8, wv), F32),
            pltpu.VMEM((HG, A_DK, A_DV), F32),
            pltpu.VMEM((_N_BMASKS, SB, SB), BF16),
            pltpu.VMEM((CPS * A_DK, SB), BF16),
        ] + stage + stage,
        compiler_params=pltpu.CompilerParams(
            dimension_semantics=("arbitrary", "arbitrary", "arbitrary"),
            vmem_limit_bytes=VMEM_LIMIT),
        name="gdn",
    )(proj, proj, proj, proj, small_t, alog_b, dtb_b, conv_a, conv_a, conv_a, norm_a)


_D_MASK, _D_Q, _D_K, _D_SB, _N_DEC = 0, 1, 2, 3, 4


def _ret_kernel(q_ref, k_ref, v_ref, g_ref, nb_ref, o_ref, state_ref, dec_ref):
    h = pl.program_id(1)
    r = pl.program_id(2)
    lb = q_ref.shape[0]
    nsb = lb // SB

    @pl.when(r == 0)
    def _():
        state_ref[...] = jnp.zeros_like(state_ref)
        hf = jnp.full((SB, SB), h, jnp.int32).astype(F32)
        lg = jnp.log1p(-jnp.exp2(-5.0 - hf))
        row = lax.broadcasted_iota(jnp.int32, (SB, SB), 0)
        col = lax.broadcasted_iota(jnp.int32, (SB, SB), 1)
        allow = ((col >> 6) <= (row >> 6)).astype(F32)
        dist = jnp.abs(row - col).astype(F32)
        rowf = row.astype(F32)
        dec_ref[_D_MASK] = jnp.exp(lg * dist) * allow * (B_DK ** -0.5)
        dec_ref[_D_Q] = jnp.exp(lg * (rowf + 1.0))
        dec_ref[_D_K] = jnp.exp(lg * (SB - 1.0 - rowf)) * (B_DK ** -0.5)
        dec_ref[_D_SB] = jnp.exp(lg * SB)

    mask = dec_ref[_D_MASK]
    q_dec = dec_ref[_D_Q]
    k_dec = dec_ref[_D_K]
    sb_dec = dec_ref[_D_SB][0:1, 0:1]
    nb = nb_ref[...]

    state = state_ref[...]
    for s in range(nsb):
        r0 = s * SB
        qb = q_ref[pl.ds(r0, SB), :]
        kb = k_ref[pl.ds(r0, SB), :]
        vb = v_ref[pl.ds(r0, SB), :]
        sc = lax.dot_general(qb, kb, _NT, preferred_element_type=F32)
        o = (_dot((sc * mask).astype(BF16), vb)
             + _dot((qb.astype(F32) * q_dec).astype(BF16), state.astype(BF16)))
        state = state * sb_dec + lax.dot_general(
            (kb.astype(F32) * k_dec).astype(BF16), vb, _TN, preferred_element_type=F32)
        ms = jnp.mean(o * o, axis=-1, keepdims=True)
        gate = g_ref[pl.ds(r0, SB), :].astype(F32)
        o_ref[pl.ds(r0, SB), :] = (o * lax.rsqrt(ms + EPS) * nb * gate).astype(o_ref.dtype)
    state_ref[...] = state


def _ret(proj, norm_b, batch, seq, lb):
    t = proj.shape[0]
    nr = seq // lb
    base = 2 * A_QK + 2 * A_V
    q_off = base // B_DK
    k_off = (base + B_QK) // B_DK
    v_off = (base + 2 * B_QK) // B_DV
    g_off = (base + 2 * B_QK + B_V) // B_DV

    def rows(b, h, r):
        return b * nr + r

    return pl.pallas_call(
        _ret_kernel,
        grid=(batch, B_HEADS, nr),
        in_specs=[
            pl.BlockSpec((lb, B_DK), lambda b, h, r: (rows(b, h, r), q_off + h)),
            pl.BlockSpec((lb, B_DK), lambda b, h, r: (rows(b, h, r), k_off + h)),
            pl.BlockSpec((lb, B_DV), lambda b, h, r: (rows(b, h, r), v_off + h)),
            pl.BlockSpec((lb, B_DV), lambda b, h, r: (rows(b, h, r), g_off + h)),
            pl.BlockSpec((1, B_DV), lambda b, h, r: (0, h)),
        ],
        out_specs=pl.BlockSpec((lb, B_DV), lambda b, h, r: (rows(b, h, r), h)),
        out_shape=jax.ShapeDtypeStruct((t, B_V), BF16),
        scratch_shapes=[pltpu.VMEM((B_DK, B_DV), F32), pltpu.VMEM((_N_DEC, SB, SB), F32)],
        compiler_params=pltpu.CompilerParams(
            dimension_semantics=("arbitrary", "arbitrary", "arbitrary"),
            vmem_limit_bytes=VMEM_LIMIT),
        name="ret",
    )(proj, proj, proj, proj, norm_b)


def _merge_kernel(oa_ref, ob_ref, ga_ref, gb_ref, x_ref, wa_ref, wb_ref, wo_ref, n_ref, o_ref):
    ya = _dot(oa_ref[...], wa_ref[...])
    yb = _dot(ob_ref[...], wb_ref[...])
    ga = jax.nn.sigmoid(ga_ref[...].astype(F32))
    gb = jax.nn.sigmoid(gb_ref[...].astype(F32))
    mix = _dot((ga * ya + gb * yb).astype(BF16), wo_ref[...])
    ms = jnp.mean(mix * mix, axis=-1, keepdims=True)
    o_ref[...] = x_ref[...] + mix * lax.rsqrt(ms + EPS) * n_ref[...]


def _merge(oa, ob, proj, x2, wa, wb, wo, gain, tm):
    t, d = x2.shape
    gate_off = (2 * A_QK + 2 * A_V + 2 * B_QK + 2 * B_V) // d
    const = lambda i: (0, 0)
    return pl.pallas_call(
        _merge_kernel,
        grid=(t // tm,),
        in_specs=[
            pl.BlockSpec((tm, A_V), lambda i: (i, 0)),
            pl.BlockSpec((tm, B_V), lambda i: (i, 0)),
            pl.BlockSpec((tm, d), lambda i: (i, gate_off)),
            pl.BlockSpec((tm, d), lambda i: (i, gate_off + 1)),
            pl.BlockSpec((tm, d), lambda i: (i, 0)),
            pl.BlockSpec((A_V, d), const),
            pl.BlockSpec((B_V, d), const),
            pl.BlockSpec((d, d), const),
            pl.BlockSpec((1, d), const),
        ],
        out_specs=pl.BlockSpec((tm, d), lambda i: (i, 0)),
        out_shape=jax.ShapeDtypeStruct((t, d), F32),
        compiler_params=pltpu.CompilerParams(
            dimension_semantics=("arbitrary",), vmem_limit_bytes=VMEM_LIMIT),
        name="merge",
    )(oa, ob, proj, proj, x2, wa, wb, wo, gain)


def _mlp_kernel(x_ref, gpre_ref, wu_ref, wd_ref, gpost_ref, o_ref, *, ff_tile):
    x = x_ref[...]
    ms = jnp.mean(x * x, axis=-1, keepdims=True)
    hn = (x * lax.rsqrt(ms + EPS) * gpre_ref[...]).astype(BF16)
    d_ff = wu_ref.shape[1]
    ff = jnp.zeros(x.shape, F32)
    for j in range(d_ff // ff_tile):
        up = _dot(hn, wu_ref[:, j * ff_tile:(j + 1) * ff_tile])
        act = jnp.square(jnp.maximum(up, 0.0)).astype(BF16)
        ff = ff + _dot(act, wd_ref[j * ff_tile:(j + 1) * ff_tile, :])
    ms2 = jnp.mean(ff * ff, axis=-1, keepdims=True)
    o_ref[...] = x + ff * lax.rsqrt(ms2 + EPS) * gpost_ref[...]


def _mlp(x1, gpre, wu, wd, gpost, tm, ff_tile):
    t, d = x1.shape
    d_ff = wu.shape[1]
    const = lambda i: (0, 0)
    return pl.pallas_call(
        functools.partial(_mlp_kernel, ff_tile=ff_tile),
        grid=(t // tm,),
        in_specs=[
            pl.BlockSpec((tm, d), lambda i: (i, 0)),
            pl.BlockSpec((1, d), const),
            pl.BlockSpec((d, d_ff), const),
            pl.BlockSpec((d_ff, d), const),
            pl.BlockSpec((1, d), const),
        ],
        out_specs=pl.BlockSpec((tm, d), lambda i: (i, 0)),
        out_shape=jax.ShapeDtypeStruct((t, d), F32),
        compiler_params=pltpu.CompilerParams(
            dimension_semantics=("arbitrary",), vmem_limit_bytes=VMEM_LIMIT),
        name="mlp",
    )(x1, gpre, wu, wd, gpost)


def _pick(n, candidates):
    for c in candidates:
        if n % c == 0:
            return c
    raise ValueError(f"no tile for {n} among {candidates}")


def _layer(x, n_mix_pre, n_mix_post, n_mlp_pre, n_mlp_post, w_in, conv_a, a_log, dt_bias,
           norm_a, norm_b, w_br_a, w_br_b, w_out, w_up, w_down):
    batch, seq, d = x.shape
    t = batch * seq
    assert seq % SB == 0
    x2 = x.reshape(t, d)

    n_small_lo = 2 * A_QK + 2 * A_V
    n_small_hi = n_small_lo + 2 * A_HEADS
    w_main = jnp.concatenate([w_in[:, :n_small_lo], w_in[:, n_small_hi:]], axis=1).astype(BF16)
    w_small = jnp.pad(w_in[:, n_small_lo:n_small_hi], ((0, 0), (0, LANES - 2 * A_HEADS))).astype(BF16)

    inv_freq = ROPE_BASE ** (-jnp.arange(0, B_DK, 2, dtype=F32) / B_DK)
    ang = jnp.arange(seq, dtype=F32)[:, None] * inv_freq[None, :]

    tm = _pick(seq, (1024, 512, 256))
    proj, small = _inproj(x2, n_mix_pre.reshape(1, d), w_main, w_small,
                          jnp.cos(ang), jnp.sin(ang), tm, seq)

    small_t = jnp.transpose(small[:, :2 * A_HEADS].reshape(batch, seq, 2 * A_HEADS), (0, 2, 1))
    alog_b = jnp.broadcast_to(a_log.reshape(A_HEADS, 1), (A_HEADS, SB))
    dtb_b = jnp.broadcast_to(dt_bias.reshape(A_HEADS, 1), (A_HEADS, SB))
    oa = _gdn(proj, small_t, alog_b, dtb_b, conv_a, norm_a.reshape(1, A_DV), batch, seq)

    lb = _pick(seq, (1024, 512, 256))
    ob = _ret(proj, norm_b.reshape(1, B_V), batch, seq, lb)

    tm2 = _pick(t, (512, 256))
    x1 = _merge(oa, ob, proj, x2, w_br_a.astype(BF16), w_br_b.astype(BF16), w_out.astype(BF16),
                n_mix_post.reshape(1, d), tm2)
    out = _mlp(x1, n_mlp_pre.reshape(1, d), w_up.astype(BF16), w_down.astype(BF16),
               n_mlp_post.reshape(1, d), tm2, 1024)
    return out.reshape(batch, seq, d)


def kernel(x, norm_mix_pre, norm_mix_post, norm_mlp_pre, norm_mlp_post, w_in, conv_a, a_log,
           dt_bias, norm_a, norm_b, w_br_a, w_br_b, w_out, w_up, w_down):
    for l in range(w_in.shape[0]):
        x = _layer(x, norm_mix_pre[l], norm_mix_post[l], norm_mlp_pre[l], norm_mlp_post[l],
                   w_in[l], conv_a[l], a_log[l], dt_bias[l], norm_a[l], norm_b[l],
                   w_br_a[l], w_br_b[l], w_out[l], w_up[l], w_down[l])
    return x
```

```python
import functools

import jax
import jax.numpy as jnp
from jax import lax
from jax.experimental import pallas as pl
from jax.experimental.pallas import tpu as pltpu

F32 = jnp.float32
BF16 = jnp.bfloat16

CHUNK = 64
EPS = 1e-6
A_HEADS, A_DK, A_DV, CONV_W = 8, 128, 256, 4
B_HEADS, B_DK, B_DV = 4, 256, 512
ROPE_BASE = 10000.0
A_QK, A_V = A_HEADS * A_DK, A_HEADS * A_DV
B_QK, B_V = B_HEADS * B_DK, B_HEADS * B_DV

SB = 256
CPS = SB // CHUNK
LANES = 128
HG = 8
VMEM_LIMIT = 56 * 1024 * 1024

_PW = 2048
_T_SPLIT = (2 * A_QK + 2 * A_V) // _PW
_T_ROT = _T_SPLIT
_T_GB = (2 * A_QK + 2 * A_V + 2 * B_QK + B_V) // _PW

_NT = (((1,), (1,)), ((), ()))
_TN = (((0,), (0,)), ((), ()))


def _dot(a, b):
    return jnp.dot(a, b, preferred_element_type=F32)


def _silu(x):
    h = 0.5 * x
    return h * (1.0 + jnp.tanh(h))


def _inproj_kernel(x_ref, g_ref, wl_ref, wr_ref, ws_ref, cl_ref, sl_ref, ch_ref, sh_ref,
                   o_ref, os_ref, xn_ref, *, tiles_per_seq):
    i = pl.program_id(0)
    j = pl.program_id(1)

    @pl.when(j == 0)
    def _():
        x = x_ref[...]
        ms = jnp.mean(x * x, axis=-1, keepdims=True)
        xn = (x * lax.rsqrt(ms + EPS) * g_ref[...]).astype(BF16)
        xn_ref[...] = xn
        os_ref[...] = _dot(xn, ws_ref[...])

    @pl.when(j < _T_SPLIT)
    def _():
        o_ref[...] = _dot(xn_ref[...], wl_ref[...]).astype(o_ref.dtype)

    @pl.when(j == _T_ROT)
    def _():
        acc = _dot(xn_ref[...], wr_ref[...])
        k = i % tiles_per_seq
        ch = ch_ref[pl.ds(k, 1), :]
        sh = sh_ref[pl.ds(k, 1), :]
        cos = ch * cl_ref[...] - sh * sl_ref[...]
        sin = sh * cl_ref[...] + ch * sl_ref[...]
        half = B_DK // 2
        for hd in range(_PW // B_DK):
            x1 = acc[:, hd * B_DK:hd * B_DK + half]
            x2 = acc[:, hd * B_DK + half:(hd + 1) * B_DK]
            o_ref[:, hd * B_DK:hd * B_DK + half] = (x1 * cos - x2 * sin).astype(o_ref.dtype)
            o_ref[:, hd * B_DK + half:(hd + 1) * B_DK] = (x1 * sin + x2 * cos).astype(o_ref.dtype)

    @pl.when(j == _T_GB)
    def _():
        o_ref[...] = _silu(_dot(xn_ref[...], wr_ref[...])).astype(o_ref.dtype)

    @pl.when((j >= _T_SPLIT) & (j != _T_ROT) & (j != _T_GB))
    def _():
        o_ref[...] = _dot(xn_ref[...], wr_ref[...]).astype(o_ref.dtype)


def _inproj(x2, gain, w_left, w_right, w_small, tables, tm, seq):
    t, d = x2.shape
    n = w_left.shape[1] + w_right.shape[1]
    assert w_left.shape[1] == _T_SPLIT * _PW and w_right.shape[1] % _PW == 0 and seq % tm == 0
    tiles_per_seq = seq // tm
    cos_lo, sin_lo, cos_hi, sin_hi = tables
    const = lambda i, j: (0, 0)
    return pl.pallas_call(
        functools.partial(_inproj_kernel, tiles_per_seq=tiles_per_seq),
        grid=(t // tm, n // _PW),
        in_specs=[
            pl.BlockSpec((tm, d), lambda i, j: (i, 0)),
            pl.BlockSpec((1, d), const),
            pl.BlockSpec((d, _PW), lambda i, j: (0, jnp.minimum(j, _T_SPLIT - 1))),
            pl.BlockSpec((d, _PW), lambda i, j: (0, jnp.maximum(j - _T_SPLIT, 0))),
            pl.BlockSpec((d, LANES), const),
            pl.BlockSpec((tm, B_DK // 2), const),
            pl.BlockSpec((tm, B_DK // 2), const),
            pl.BlockSpec((tiles_per_seq, B_DK // 2), const),
            pl.BlockSpec((tiles_per_seq, B_DK // 2), const),
        ],
        out_specs=[
            pl.BlockSpec((tm, _PW), lambda i, j: (i, j)),
            pl.BlockSpec((tm, LANES), lambda i, j: (i, 0)),
        ],
        out_shape=[
            jax.ShapeDtypeStruct((t, n), BF16),
            jax.ShapeDtypeStruct((t, LANES), F32),
        ],
        scratch_shapes=[pltpu.VMEM((tm, d), BF16)],
        compiler_params=pltpu.CompilerParams(
            dimension_semantics=("arbitrary", "arbitrary"), vmem_limit_bytes=VMEM_LIMIT),
        name="inproj",
    )(x2, gain, w_left, w_right, w_small, cos_lo, sin_lo, cos_hi, sin_hi)


def _conv_silu(tail_ref, x_ref, c_ref):
    x = x_ref[...].astype(F32)
    rows = x.shape[0]
    xe = jnp.concatenate([tail_ref[...], x], axis=0)
    tail_ref[...] = x[rows - 8:, :]
    c_half = 0.5 * c_ref[...]
    h = x * c_half[CONV_W - 1:CONV_W, :]
    for d in range(1, CONV_W):
        h = h + pltpu.roll(xe, d, axis=0)[8:, :] * c_half[CONV_W - 1 - d:CONV_W - d, :]
    return h * (1.0 + jnp.tanh(h))


def _l2norm(y):
    return y * lax.rsqrt(jnp.sum(y * y, axis=-1, keepdims=True) + EPS)


def _group_cumsum_rows(x, group):
    lane = lax.broadcasted_iota(jnp.int32, x.shape, 1)
    pos = lane & (group - 1)
    s = 1
    while s < group:
        shifted = pltpu.roll(x, s, axis=1)
        x = x + jnp.where(pos >= s, shifted, 0.0)
        s *= 2
    return x


_B_NEG16, _B_C1, _B_C2, _B_EYE, _B_CAUSAL, _N_BMASKS = 0, 1, 2, 3, 4, 5


def _col_replicated(row):
    return jnp.transpose(jnp.broadcast_to(row, (LANES, SB)))


def _gdn_prep(grp, q_ref, k_ref, v_ref, st_ref, alog_ref, dtb_ref, cq_ref, ck_ref, cv_ref,
              eq_ref, ek_ref, ev_ref, dst):
    lq_ref, kk_ref, rhs_ref, qe_ref, kdt_ref, gcc_ref, grs_ref = dst
    qa = _conv_silu(eq_ref, q_ref, cq_ref)
    ka = _conv_silu(ek_ref, k_ref, ck_ref)
    va = _conv_silu(ev_ref, v_ref, cv_ref)
    for j in range(HG):
        hh = grp * HG + j
        beta_row = jax.nn.sigmoid(st_ref[pl.ds(hh, 1), :])
        xg = st_ref[pl.ds(A_HEADS + hh, 1), :] + dtb_ref[pl.ds(hh, 1), :]
        softplus = jnp.maximum(xg, 0.0) + jnp.log1p(jnp.exp(-jnp.abs(xg)))
        g_row = -jnp.exp(alog_ref[pl.ds(hh, 1), :]) * softplus
        gc_row8 = _group_cumsum_rows(jnp.broadcast_to(g_row, (8, SB)), CHUNK)
        gc = _col_replicated(gc_row8[0:1])
        bc = _col_replicated(beta_row)
        grs_ref[j] = gc_row8
        gcc_ref[j] = gc

        q = _l2norm(qa[:, j * A_DK:(j + 1) * A_DK]) * (A_DK ** -0.5)
        k = _l2norm(ka[:, j * A_DK:(j + 1) * A_DK])
        v = va[:, j * A_DV:(j + 1) * A_DV]
        eg = jnp.exp(gc)
        kb = k * bc
        glast_rep = jnp.concatenate(
            [jnp.broadcast_to(gc[c * CHUNK + CHUNK - 1:(c + 1) * CHUNK, :], (CHUNK, LANES))
             for c in range(CPS)], axis=0)
        lq_ref[j] = jnp.concatenate([kb, q], axis=0).astype(BF16)
        kk_ref[j] = k.astype(BF16)
        rhs_ref[j] = jnp.concatenate([v * jnp.concatenate([bc, bc], axis=1), kb * eg],
                                     axis=1).astype(BF16)
        qe_ref[j] = (q * eg).astype(BF16)
        kdt_ref[j] = jnp.transpose(k * jnp.exp(glast_rep - gc)).astype(BF16)


def _gdn_main(fresh, src, z_ref, na_ref, o_ref, state_ref, bmask_ref, sel_ref):
    lq_ref, kk_ref, rhs_ref, qe_ref, kdt_ref, gcc_ref, grs_ref = src
    heads = range(HG)

    kq = [lax.dot_general(lq_ref[j], kk_ref[j], _NT, preferred_element_type=F32)
          for j in heads]

    def masked_scores(j):
        gc = gcc_ref[j]
        decay = jnp.exp(jnp.minimum(jnp.concatenate([gc, gc], axis=1) - grs_ref[j][0:1], 0.0))
        ab = (kq[j][:SB] * decay).astype(BF16)
        n1b = ab * bmask_ref[_B_NEG16]
        return ((kq[j][SB:] * decay).astype(BF16) * bmask_ref[_B_CAUSAL], n1b,
                bmask_ref[_B_EYE] + n1b, ab * bmask_ref[_B_C1], ab * bmask_ref[_B_C2])

    attn, n1b, pb, c1b, c2b = zip(*[masked_scores(j) for j in heads])
    eye = bmask_ref[_B_EYE]

    n2b = [_dot(n1b[j], n1b[j]).astype(BF16) for j in heads]
    pb = [_dot(pb[j], eye + n2b[j]).astype(BF16) for j in heads]
    n4b = [_dot(n2b[j], n2b[j]).astype(BF16) for j in heads]
    pb = [_dot(pb[j], eye + n4b[j]).astype(BF16) for j in heads]
    n8b = [_dot(n4b[j], n4b[j]).astype(BF16) for j in heads]
    pb = [_dot(pb[j], eye + n8b[j]).astype(BF16) for j in heads]
    tinv = pb
    for blk, cb in ((16, c1b), (32, c2b)):
        low = lambda x: jnp.concatenate(
            [x[b + blk:b + 2 * blk] for b in range(0, SB, 2 * blk)], axis=0)
        eye_low = low(eye)
        xm = [eye_low - _dot(low(tinv[j]), cb[j]).astype(BF16) for j in heads]
        t_low = [_dot(xm[j], tinv[j]).astype(BF16) for j in heads]
        tinv = [jnp.concatenate(
            [piece for i, b in enumerate(range(0, SB, 2 * blk))
             for piece in (tinv[j][b:b + blk], t_low[j][i * blk:(i + 1) * blk])], axis=0)
            for j in heads]

    uwb = [_dot(tinv[j], rhs_ref[j]).astype(BF16) for j in heads]
    aw = [_dot(attn[j], uwb[j]) for j in heads]
    qeff = [(qe_ref[j].astype(F32) - aw[j][:, A_DV:]).astype(BF16) for j in heads]
    bp = [_dot(jnp.concatenate([kdt_ref[j]] * CPS, axis=0) * sel_ref[...], uwb[j])
          for j in heads]

    st = [jnp.where(fresh, 0.0, state_ref[j]) for j in heads]
    outs = [[] for _ in heads]
    for c in range(CPS):
        for j in heads:
            blk = bp[j][c * A_DK:(c + 1) * A_DK]
            lhs = jnp.concatenate([blk[:, A_DV:].astype(BF16), qeff[j][c * CHUNK:(c + 1) * CHUNK]],
                                  axis=0)
            res = _dot(lhs, st[j].astype(BF16))
            outs[j].append(res[A_DK:])
            ld = jnp.exp(gcc_ref[j, c * CHUNK + CHUNK - 1:(c + 1) * CHUNK, :])
            st[j] = st[j] * jnp.concatenate([ld, ld], axis=1) + blk[:, :A_DV] - res[:A_DK]

    na = na_ref[...]
    for j in heads:
        state_ref[j] = st[j]
        o = jnp.concatenate(outs[j], axis=0) + aw[j][:, :A_DV]
        ms = jnp.mean(o * o, axis=-1, keepdims=True)
        z = z_ref[:, j * A_DV:(j + 1) * A_DV].astype(F32)
        o_ref[:, j * A_DV:(j + 1) * A_DV] = (
            o * lax.rsqrt(ms + EPS) * na * _silu(z)).astype(o_ref.dtype)


def _gdn_kernel(q_ref, k_ref, v_ref, z_ref, st_ref, alog_ref, dtb_ref, cq_ref, ck_ref, cv_ref,
                na_ref, o_ref, eq_ref, ek_ref, ev_ref, state_ref, bmask_ref, sel_ref, *stage):
    grp = pl.program_id(1)
    r = pl.program_id(2)
    stage_a, stage_b = stage[:len(stage) // 2], stage[len(stage) // 2:]

    @pl.when((pl.program_id(0) == 0) & (grp == 0) & (r == 0))
    def _():
        row = lax.broadcasted_iota(jnp.int32, (SB, SB), 0)
        col = lax.broadcasted_iota(jnp.int32, (SB, SB), 1)

        def same(bits):
            return ((row >> bits) == (col >> bits)).astype(F32)

        m16, m32, m64 = same(4), same(5), same(6)
        lower = (row >= col).astype(F32)
        bmask_ref[_B_CAUSAL] = (m64 * lower).astype(BF16)
        bmask_ref[_B_NEG16] = (-(m16 * (row > col).astype(F32))).astype(BF16)
        bmask_ref[_B_C1] = ((m32 - m16) * lower).astype(BF16)
        bmask_ref[_B_C2] = ((m64 - m32) * lower).astype(BF16)
        bmask_ref[_B_EYE] = (row == col).astype(F32).astype(BF16)
        rsel = lax.broadcasted_iota(jnp.int32, (CPS * A_DK, SB), 0)
        csel = lax.broadcasted_iota(jnp.int32, (CPS * A_DK, SB), 1)
        sel_ref[...] = ((rsel >> 7) == (csel >> 6)).astype(F32).astype(BF16)
        for ref in stage_b:
            ref[...] = jnp.zeros_like(ref)
        state_ref[...] = jnp.zeros_like(state_ref)

    @pl.when(r == 0)
    def _():
        eq_ref[...] = jnp.zeros_like(eq_ref)
        ek_ref[...] = jnp.zeros_like(ek_ref)
        ev_ref[...] = jnp.zeros_like(ev_ref)

    def step(dst, src):
        _gdn_prep(grp, q_ref, k_ref, v_ref, st_ref, alog_ref, dtb_ref, cq_ref, ck_ref, cv_ref,
                  eq_ref, ek_ref, ev_ref, dst)
        _gdn_main(r <= 1, src, z_ref, na_ref, o_ref, state_ref, bmask_ref, sel_ref)

    @pl.when(r % 2 == 0)
    def _():
        step(stage_a, stage_b)

    @pl.when(r % 2 == 1)
    def _():
        step(stage_b, stage_a)


def _gdn(proj, small_t, alog_b, dtb_b, conv_a, norm_a, batch, seq):
    t = proj.shape[0]
    nr = seq // SB
    wq, wv = HG * A_DK, HG * A_DV
    k_off = A_QK // wq
    v_off = (2 * A_QK) // wv
    z_off = (2 * A_QK + A_V) // wv

    def staged(b, g, r):
        return b * nr + jnp.minimum(r, nr - 1)

    def solved(b, g, r):
        return b * nr + jnp.maximum(r - 1, 0)

    stage = [
        pltpu.VMEM((HG, 2 * SB, A_DK), BF16),
        pltpu.VMEM((HG, SB, A_DK), BF16),
        pltpu.VMEM((HG, SB, A_DV + A_DK), BF16),
        pltpu.VMEM((HG, SB, A_DK), BF16),
        pltpu.VMEM((HG, A_DK, SB), BF16),
        pltpu.VMEM((HG, SB, LANES), F32),
        pltpu.VMEM((HG, 8, SB), F32),
    ]
    return pl.pallas_call(
        _gdn_kernel,
        grid=(batch, A_HEADS // HG, nr + 1),
        in_specs=[
            pl.BlockSpec((SB, wq), lambda b, g, r: (staged(b, g, r), g)),
            pl.BlockSpec((SB, wq), lambda b, g, r: (staged(b, g, r), k_off + g)),
            pl.BlockSpec((SB, wv), lambda b, g, r: (staged(b, g, r), v_off + g)),
            pl.BlockSpec((SB, wv), lambda b, g, r: (solved(b, g, r), z_off + g)),
            pl.BlockSpec((None, 2 * A_HEADS, SB), lambda b, g, r: (b, 0, jnp.minimum(r, nr - 1))),
            pl.BlockSpec((A_HEADS, SB), lambda b, g, r: (0, 0)),
            pl.BlockSpec((A_HEADS, SB), lambda b, g, r: (0, 0)),
            pl.BlockSpec((CONV_W, wq), lambda b, g, r: (0, g)),
            pl.BlockSpec((CONV_W, wq), lambda b, g, r: (0, k_off + g)),
            pl.BlockSpec((CONV_W, wv), lambda b, g, r: (0, v_off + g)),
            pl.BlockSpec((1, A_DV), lambda b, g, r: (0, 0)),
        ],
        out_specs=pl.BlockSpec((SB, wv), lambda b, g, r: (solved(b, g, r), g)),
        out_shape=jax.ShapeDtypeStruct((t, A_V), BF16),
        scratch_shapes=[
            pltpu.VMEM((8, wq), F32),
            pltpu.VMEM((8, wq), F32),
            pltpu.VMEM((8, wv), F32),
            pltpu.VMEM((HG, A_DK, A_DV), F32),
            pltpu.VMEM((_N_BMASKS, SB, SB), BF16),
            pltpu.VMEM((CPS * A_DK, SB), BF16),
        ] + stage + stage,
        compiler_params=pltpu.CompilerParams(
            dimension_semantics=("arbitrary", "arbitrary", "arbitrary"),
            vmem_limit_bytes=VMEM_LIMIT),
        name="gdn",
    )(proj, proj, proj, proj, small_t, alog_b, dtb_b, conv_a, conv_a, conv_a, norm_a)


_D_MASK, _D_Q, _D_K, _D_SB, _N_DEC = 0, 1, 2, 3, 4


def _ret_kernel(q_ref, k_ref, v_ref, g_ref, nb_ref, o_ref, state_ref, dec_ref):
    h = pl.program_id(1)
    r = pl.program_id(2)
    lb = q_ref.shape[0]
    nsb = lb // SB

    @pl.when(r == 0)
    def _():
        state_ref[...] = jnp.zeros_like(state_ref)
        hf = jnp.full((SB, SB), h, jnp.int32).astype(F32)
        lg = jnp.log1p(-jnp.exp2(-5.0 - hf))
        row = lax.broadcasted_iota(jnp.int32, (SB, SB), 0)
        col = lax.broadcasted_iota(jnp.int32, (SB, SB), 1)
        allow = ((col >> 6) <= (row >> 6)).astype(F32)
        dist = jnp.abs(row - col).astype(F32)
        rowf = row.astype(F32)
        dec_ref[_D_MASK] = jnp.exp(lg * dist) * allow * (B_DK ** -0.5)
        dec_ref[_D_Q] = jnp.exp(lg * (rowf + 1.0))
        dec_ref[_D_K] = jnp.exp(lg * (SB - 1.0 - rowf)) * (B_DK ** -0.5)
        dec_ref[_D_SB] = jnp.exp(lg * SB)

    mask = dec_ref[_D_MASK]
    q_dec = dec_ref[_D_Q]
    k_dec = dec_ref[_D_K]
    sb_dec = dec_ref[_D_SB][0:1, 0:1]
    nb = nb_ref[...]

    state = state_ref[...]
    for s in range(nsb):
        r0 = s * SB
        qb = q_ref[pl.ds(r0, SB), :]
        kb = k_ref[pl.ds(r0, SB), :]
        vb = v_ref[pl.ds(r0, SB), :]
        sc = lax.dot_general(qb, kb, _NT, preferred_element_type=F32)
        o = (_dot((sc * mask).astype(BF16), vb)
             + _dot((qb.astype(F32) * q_dec).astype(BF16), state.astype(BF16)))
        state = state * sb_dec + lax.dot_general(
            (kb.astype(F32) * k_dec).astype(BF16), vb, _TN, preferred_element_type=F32)
        ms = jnp.mean(o * o, axis=-1, keepdims=True)
        gate = g_ref[pl.ds(r0, SB), :].astype(F32)
        o_ref[pl.ds(r0, SB), :] = (o * lax.rsqrt(ms + EPS) * nb * gate).astype(o_ref.dtype)
    state_ref[...] = state


def _ret(proj, norm_b, batch, seq, lb):
    t = proj.shape[0]
    nr = seq // lb
    base = 2 * A_QK + 2 * A_V
    q_off = base // B_DK
    k_off = (base + B_QK) // B_DK
    v_off = (base + 2 * B_QK) // B_DV
    g_off = (base + 2 * B_QK + B_V) // B_DV

    def rows(b, h, r):
        return b * nr + r

    return pl.pallas_call(
        _ret_kernel,
        grid=(batch, B_HEADS, nr),
        in_specs=[
            pl.BlockSpec((lb, B_DK), lambda b, h, r: (rows(b, h, r), q_off + h)),
            pl.BlockSpec((lb, B_DK), lambda b, h, r: (rows(b, h, r), k_off + h)),
            pl.BlockSpec((lb, B_DV), lambda b, h, r: (rows(b, h, r), v_off + h)),
            pl.BlockSpec((lb, B_DV), lambda b, h, r: (rows(b, h, r), g_off + h)),
            pl.BlockSpec((1, B_DV), lambda b, h, r: (0, h)),
        ],
        out_specs=pl.BlockSpec((lb, B_DV), lambda b, h, r: (rows(b, h, r), h)),
        out_shape=jax.ShapeDtypeStruct((t, B_V), BF16),
        scratch_shapes=[pltpu.VMEM((B_DK, B_DV), F32), pltpu.VMEM((_N_DEC, SB, SB), F32)],
        compiler_params=pltpu.CompilerParams(
            dimension_semantics=("arbitrary", "arbitrary", "arbitrary"),
            vmem_limit_bytes=VMEM_LIMIT),
        name="ret",
    )(proj, proj, proj, proj, norm_b)


def _merge_kernel(oa_ref, ob_ref, ga_ref, gb_ref, x_ref, wa_ref, wb_ref, wo_ref, n_ref, o_ref):
    ya = _dot(oa_ref[...], wa_ref[...])
    yb = _dot(ob_ref[...], wb_ref[...])
    ga = jax.nn.sigmoid(ga_ref[...].astype(F32))
    gb = jax.nn.sigmoid(gb_ref[...].astype(F32))
    mix = _dot((ga * ya + gb * yb).astype(BF16), wo_ref[...])
    ms = jnp.mean(mix * mix, axis=-1, keepdims=True)
    o_ref[...] = x_ref[...] + mix * lax.rsqrt(ms + EPS) * n_ref[...]


def _merge(oa, ob, proj, x2, wa, wb, wo, gain, tm):
    t, d = x2.shape
    gate_off = (2 * A_QK + 2 * A_V + 2 * B_QK + 2 * B_V) // d
    const = lambda i: (0, 0)
    return pl.pallas_call(
        _merge_kernel,
        grid=(t // tm,),
        in_specs=[
            pl.BlockSpec((tm, A_V), lambda i: (i, 0)),
            pl.BlockSpec((tm, B_V), lambda i: (i, 0)),
            pl.BlockSpec((tm, d), lambda i: (i, gate_off)),
            pl.BlockSpec((tm, d), lambda i: (i, gate_off + 1)),
            pl.BlockSpec((tm, d), lambda i: (i, 0)),
            pl.BlockSpec((A_V, d), const),
            pl.BlockSpec((B_V, d), const),
            pl.BlockSpec((d, d), const),
            pl.BlockSpec((1, d), const),
        ],
        out_specs=pl.BlockSpec((tm, d), lambda i: (i, 0)),
        out_shape=jax.ShapeDtypeStruct((t, d), F32),
        compiler_params=pltpu.CompilerParams(
            dimension_semantics=("arbitrary",), vmem_limit_bytes=VMEM_LIMIT),
        name="merge",
    )(oa, ob, proj, proj, x2, wa, wb, wo, gain)


def _mlp_kernel(x_ref, gpre_ref, wu_ref, wd_ref, gpost_ref, o_ref, *, ff_tile):
    x = x_ref[...]
    ms = jnp.mean(x * x, axis=-1, keepdims=True)
    hn = (x * lax.rsqrt(ms + EPS) * gpre_ref[...]).astype(BF16)
    d_ff = wu_ref.shape[1]
    ff = jnp.zeros(x.shape, F32)
    for j in range(d_ff // ff_tile):
        up = _dot(hn, wu_ref[:, j * ff_tile:(j + 1) * ff_tile])
        act = jnp.square(jnp.maximum(up, 0.0)).astype(BF16)
        ff = ff + _dot(act, wd_ref[j * ff_tile:(j + 1) * ff_tile, :])
    ms2 = jnp.mean(ff * ff, axis=-1, keepdims=True)
    o_ref[...] = x + ff * lax.rsqrt(ms2 + EPS) * gpost_ref[...]


def _mlp(x1, gpre, wu, wd, gpost, tm, ff_tile):
    t, d = x1.shape
    d_ff = wu.shape[1]
    const = lambda i: (0, 0)
    return pl.pallas_call(
        functools.partial(_mlp_kernel, ff_tile=ff_tile),
        grid=(t // tm,),
        in_specs=[
            pl.BlockSpec((tm, d), lambda i: (i, 0)),
            pl.BlockSpec((1, d), const),
            pl.BlockSpec((d, d_ff), const),
            pl.BlockSpec((d_ff, d), const),
            pl.BlockSpec((1, d), const),
        ],
        out_specs=pl.BlockSpec((tm, d), lambda i: (i, 0)),
        out_shape=jax.ShapeDtypeStruct((t, d), F32),
        compiler_params=pltpu.CompilerParams(
            dimension_semantics=("arbitrary",), vmem_limit_bytes=VMEM_LIMIT),
        name="mlp",
    )(x1, gpre, wu, wd, gpost)


def _pick(n, candidates):
    for c in candidates:
        if n % c == 0:
            return c
    raise ValueError(f"no tile for {n} among {candidates}")


def _layer(x, n_mix_pre, n_mix_post, n_mlp_pre, n_mlp_post, w_in, conv_a, a_log, dt_bias,
           norm_a, norm_b, w_br_a, w_br_b, w_out, w_up, w_down):
    batch, seq, d = x.shape
    t = batch * seq
    assert seq % SB == 0
    x2 = x.reshape(t, d)

    n_small_lo = 2 * A_QK + 2 * A_V
    n_small_hi = n_small_lo + 2 * A_HEADS
    w16 = w_in.astype(BF16)
    w_left = w16[:, :n_small_lo]
    w_right = w16[:, n_small_hi:]
    w_small = jnp.pad(w16[:, n_small_lo:n_small_hi], ((0, 0), (0, LANES - 2 * A_HEADS)))

    tm = _pick(seq, (1024, 512, 256))
    inv_freq = ROPE_BASE ** (-jnp.arange(0, B_DK, 2, dtype=F32) / B_DK)
    ang_lo = jnp.arange(tm, dtype=F32)[:, None] * inv_freq[None, :]
    ang_hi = (jnp.arange(seq // tm, dtype=F32) * tm)[:, None] * inv_freq[None, :]
    tables = (jnp.cos(ang_lo), jnp.sin(ang_lo), jnp.cos(ang_hi), jnp.sin(ang_hi))

    proj, small = _inproj(x2, n_mix_pre.reshape(1, d), w_left, w_right, w_small, tables, tm, seq)

    small_t = jnp.transpose(small[:, :2 * A_HEADS].reshape(batch, seq, 2 * A_HEADS), (0, 2, 1))
    alog_b = jnp.broadcast_to(a_log.reshape(A_HEADS, 1), (A_HEADS, SB))
    dtb_b = jnp.broadcast_to(dt_bias.reshape(A_HEADS, 1), (A_HEADS, SB))
    oa = _gdn(proj, small_t, alog_b, dtb_b, conv_a, norm_a.reshape(1, A_DV), batch, seq)

    lb = _pick(seq, (2048, 1024, 512, 256))
    ob = _ret(proj, norm_b.reshape(1, B_V), batch, seq, lb)

    tm2 = _pick(t, (512, 256))
    x1 = _merge(oa, ob, proj, x2, w_br_a.astype(BF16), w_br_b.astype(BF16), w_out.astype(BF16),
                n_mix_post.reshape(1, d), tm2)
    out = _mlp(x1, n_mlp_pre.reshape(1, d), w_up.astype(BF16), w_down.astype(BF16),
               n_mlp_post.reshape(1, d), tm2, 1024)
    return out.reshape(batch, seq, d)


def kernel(x, norm_mix_pre, norm_mix_post, norm_mlp_pre, norm_mlp_post, w_in, conv_a, a_log,
           dt_bias, norm_a, norm_b, w_br_a, w_br_b, w_out, w_up, w_down):
    for l in range(w_in.shape[0]):
        x = _layer(x, norm_mix_pre[l], norm_mix_post[l], norm_mlp_pre[l], norm_mlp_post[l],
                   w_in[l], conv_a[l], a_log[l], dt_bias[l], norm_a[l], norm_b[l],
                   w_br_a[l], w_br_b[l], w_out[l], w_up[l], w_down[l])
    return x
```

```python
import functools

import jax
import jax.numpy as jnp
from jax import lax
from jax.experimental import pallas as pl
from jax.experimental.pallas import tpu as pltpu

F32 = jnp.float32
BF16 = jnp.bfloat16

CHUNK = 64
EPS = 1e-6
A_HEADS, A_DK, A_DV, CONV_W = 8, 128, 256, 4
B_HEADS, B_DK, B_DV = 4, 256, 512
ROPE_BASE = 10000.0
A_QK, A_V = A_HEADS * A_DK, A_HEADS * A_DV
B_QK, B_V = B_HEADS * B_DK, B_HEADS * B_DV

SB = 256
CPS = SB // CHUNK
LANES = 128
HG = 8
VMEM_LIMIT = 56 * 1024 * 1024

_PW = 2048
_T_SPLIT = (2 * A_QK + 2 * A_V) // _PW
_T_ROT = _T_SPLIT
_T_GB = (2 * A_QK + 2 * A_V + 2 * B_QK + B_V) // _PW

_NT = (((1,), (1,)), ((), ()))
_TN = (((0,), (0,)), ((), ()))


def _dot(a, b):
    return jnp.dot(a, b, preferred_element_type=F32)


def _silu(x):
    h = 0.5 * x
    return h * (1.0 + jnp.tanh(h))


def _inproj_kernel(x_ref, g_ref, wl_ref, wr_ref, ws_ref, cl_ref, sl_ref, ch_ref, sh_ref,
                   o_ref, os_ref, xn_ref, *, tiles_per_seq):
    i = pl.program_id(0)
    j = pl.program_id(1)

    @pl.when(j == 0)
    def _():
        x = x_ref[...]
        ms = jnp.mean(x * x, axis=-1, keepdims=True)
        xn = (x * lax.rsqrt(ms + EPS) * g_ref[...]).astype(BF16)
        xn_ref[...] = xn
        os_ref[...] = _dot(xn, ws_ref[...])

    @pl.when(j < _T_SPLIT)
    def _():
        o_ref[...] = _dot(xn_ref[...], wl_ref[...]).astype(o_ref.dtype)

    @pl.when(j == _T_ROT)
    def _():
        acc = _dot(xn_ref[...], wr_ref[...])
        k = i % tiles_per_seq
        ch = ch_ref[pl.ds(k, 1), :]
        sh = sh_ref[pl.ds(k, 1), :]
        cos = ch * cl_ref[...] - sh * sl_ref[...]
        sin = sh * cl_ref[...] + ch * sl_ref[...]
        half = B_DK // 2
        for hd in range(_PW // B_DK):
            x1 = acc[:, hd * B_DK:hd * B_DK + half]
            x2 = acc[:, hd * B_DK + half:(hd + 1) * B_DK]
            o_ref[:, hd * B_DK:hd * B_DK + half] = (x1 * cos - x2 * sin).astype(o_ref.dtype)
            o_ref[:, hd * B_DK + half:(hd + 1) * B_DK] = (x1 * sin + x2 * cos).astype(o_ref.dtype)

    @pl.when(j == _T_GB)
    def _():
        o_ref[...] = _silu(_dot(xn_ref[...], wr_ref[...])).astype(o_ref.dtype)

    @pl.when((j >= _T_SPLIT) & (j != _T_ROT) & (j != _T_GB))
    def _():
        o_ref[...] = _dot(xn_ref[...], wr_ref[...]).astype(o_ref.dtype)


def _inproj(x2, gain, w_left, w_right, w_small, tables, tm, seq):
    t, d = x2.shape
    n_right = w_right.shape[1] // _PW
    n = (_T_SPLIT + n_right) * _PW
    assert w_left.shape[1] >= _T_SPLIT * _PW and w_right.shape[1] % _PW == 0 and seq % tm == 0
    tiles_per_seq = seq // tm
    cos_lo, sin_lo, cos_hi, sin_hi = tables
    const = lambda i, j: (0, 0)
    return pl.pallas_call(
        functools.partial(_inproj_kernel, tiles_per_seq=tiles_per_seq),
        grid=(t // tm, n // _PW),
        in_specs=[
            pl.BlockSpec((tm, d), lambda i, j: (i, 0)),
            pl.BlockSpec((1, d), const),
            pl.BlockSpec((d, _PW), lambda i, j: (0, jnp.minimum(j, _T_SPLIT - 1))),
            pl.BlockSpec((d, _PW), lambda i, j: (0, jnp.where(j < _T_SPLIT, n_right - 1, j - _T_SPLIT))),
            pl.BlockSpec((d, LANES), const),
            pl.BlockSpec((tm, B_DK // 2), const),
            pl.BlockSpec((tm, B_DK // 2), const),
            pl.BlockSpec((tiles_per_seq, B_DK // 2), const),
            pl.BlockSpec((tiles_per_seq, B_DK // 2), const),
        ],
        out_specs=[
            pl.BlockSpec((tm, _PW), lambda i, j: (i, j)),
            pl.BlockSpec((tm, LANES), lambda i, j: (i, 0)),
        ],
        out_shape=[
            jax.ShapeDtypeStruct((t, n), BF16),
            jax.ShapeDtypeStruct((t, LANES), F32),
        ],
        scratch_shapes=[pltpu.VMEM((tm, d), BF16)],
        compiler_params=pltpu.CompilerParams(
            dimension_semantics=("arbitrary", "arbitrary"), vmem_limit_bytes=VMEM_LIMIT),
        name="inproj",
    )(x2, gain, w_left, w_right, w_small, cos_lo, sin_lo, cos_hi, sin_hi)


def _conv_silu(tail_ref, x_ref, c_ref):
    x = x_ref[...].astype(F32)
    rows = x.shape[0]
    xe = jnp.concatenate([tail_ref[...], x], axis=0)
    tail_ref[...] = x[rows - 8:, :]
    c_half = 0.5 * c_ref[...]
    h = x * c_half[CONV_W - 1:CONV_W, :]
    for d in range(1, CONV_W):
        h = h + pltpu.roll(xe, d, axis=0)[8:, :] * c_half[CONV_W - 1 - d:CONV_W - d, :]
    return h * (1.0 + jnp.tanh(h))


def _l2norm(y):
    return y * lax.rsqrt(jnp.sum(y * y, axis=-1, keepdims=True) + EPS)


def _group_cumsum_rows(x, group):
    lane = lax.broadcasted_iota(jnp.int32, x.shape, 1)
    pos = lane & (group - 1)
    s = 1
    while s < group:
        shifted = pltpu.roll(x, s, axis=1)
        x = x + jnp.where(pos >= s, shifted, 0.0)
        s *= 2
    return x


_B_NEG16, _B_C1, _B_C2, _B_EYE, _B_CAUSAL, _N_BMASKS = 0, 1, 2, 3, 4, 5


def _col_replicated(row):
    return jnp.transpose(jnp.broadcast_to(row, (LANES, SB)))


def _gdn_prep(grp, q_ref, k_ref, v_ref, st_ref, alog_ref, dtb_ref, cq_ref, ck_ref, cv_ref,
              eq_ref, ek_ref, ev_ref, dst):
    lq_ref, kk_ref, rhs_ref, qe_ref, kdt_ref, gcc_ref, grs_ref = dst
    qa = _conv_silu(eq_ref, q_ref, cq_ref)
    ka = _conv_silu(ek_ref, k_ref, ck_ref)
    va = _conv_silu(ev_ref, v_ref, cv_ref)
    for j in range(HG):
        hh = grp * HG + j
        beta_row = jax.nn.sigmoid(st_ref[pl.ds(hh, 1), :])
        xg = st_ref[pl.ds(A_HEADS + hh, 1), :] + dtb_ref[pl.ds(hh, 1), :]
        softplus = jnp.maximum(xg, 0.0) + jnp.log1p(jnp.exp(-jnp.abs(xg)))
        g_row = -jnp.exp(alog_ref[pl.ds(hh, 1), :]) * softplus
        gc_row8 = _group_cumsum_rows(jnp.broadcast_to(g_row, (8, SB)), CHUNK)
        gc = _col_replicated(gc_row8[0:1])
        bc = _col_replicated(beta_row)
        grs_ref[j] = gc_row8
        gcc_ref[j] = gc

        q = _l2norm(qa[:, j * A_DK:(j + 1) * A_DK]) * (A_DK ** -0.5)
        k = _l2norm(ka[:, j * A_DK:(j + 1) * A_DK])
        v = va[:, j * A_DV:(j + 1) * A_DV]
        eg = jnp.exp(gc)
        kb = k * bc
        glast_rep = jnp.concatenate(
            [jnp.broadcast_to(gc[c * CHUNK + CHUNK - 1:(c + 1) * CHUNK, :], (CHUNK, LANES))
             for c in range(CPS)], axis=0)
        lq_ref[j] = jnp.concatenate([kb, q], axis=0).astype(BF16)
        kk_ref[j] = k.astype(BF16)
        rhs_ref[j] = jnp.concatenate([v * jnp.concatenate([bc, bc], axis=1), kb * eg],
                                     axis=1).astype(BF16)
        qe_ref[j] = (q * eg).astype(BF16)
        kdt_ref[j] = jnp.transpose(k * jnp.exp(glast_rep - gc)).astype(BF16)


def _gdn_main(fresh, src, z_ref, na_ref, o_ref, state_ref, bmask_ref, sel_ref):
    lq_ref, kk_ref, rhs_ref, qe_ref, kdt_ref, gcc_ref, grs_ref = src
    heads = range(HG)

    kq = [lax.dot_general(lq_ref[j], kk_ref[j], _NT, preferred_element_type=F32)
          for j in heads]

    def masked_scores(j):
        gc = gcc_ref[j]
        decay = jnp.exp(jnp.minimum(jnp.concatenate([gc, gc], axis=1) - grs_ref[j][0:1], 0.0))
        ab = (kq[j][:SB] * decay).astype(BF16)
        n1b = ab * bmask_ref[_B_NEG16]
        return ((kq[j][SB:] * decay).astype(BF16) * bmask_ref[_B_CAUSAL], n1b,
                bmask_ref[_B_EYE] + n1b, ab * bmask_ref[_B_C1], ab * bmask_ref[_B_C2])

    attn, n1b, pb, c1b, c2b = zip(*[masked_scores(j) for j in heads])
    eye = bmask_ref[_B_EYE]

    n2b = [_dot(n1b[j], n1b[j]).astype(BF16) for j in heads]
    pb = [_dot(pb[j], eye + n2b[j]).astype(BF16) for j in heads]
    n4b = [_dot(n2b[j], n2b[j]).astype(BF16) for j in heads]
    pb = [_dot(pb[j], eye + n4b[j]).astype(BF16) for j in heads]
    n8b = [_dot(n4b[j], n4b[j]).astype(BF16) for j in heads]
    pb = [_dot(pb[j], eye + n8b[j]).astype(BF16) for j in heads]
    tinv = pb
    for blk, cb in ((16, c1b), (32, c2b)):
        low = lambda x: jnp.concatenate(
            [x[b + blk:b + 2 * blk] for b in range(0, SB, 2 * blk)], axis=0)
        eye_low = low(eye)
        xm = [eye_low - _dot(low(tinv[j]), cb[j]).astype(BF16) for j in heads]
        t_low = [_dot(xm[j], tinv[j]).astype(BF16) for j in heads]
        tinv = [jnp.concatenate(
            [piece for i, b in enumerate(range(0, SB, 2 * blk))
             for piece in (tinv[j][b:b + blk], t_low[j][i * blk:(i + 1) * blk])], axis=0)
            for j in heads]

    uwb = [_dot(tinv[j], rhs_ref[j]).astype(BF16) for j in heads]
    aw = [_dot(attn[j], uwb[j]) for j in heads]
    qeff = [(qe_ref[j].astype(F32) - aw[j][:, A_DV:]).astype(BF16) for j in heads]
    bp = [_dot(jnp.concatenate([kdt_ref[j]] * CPS, axis=0) * sel_ref[...], uwb[j])
          for j in heads]

    st = [jnp.where(fresh, 0.0, state_ref[j]) for j in heads]
    outs = [[] for _ in heads]
    for c in range(CPS):
        for j in heads:
            blk = bp[j][c * A_DK:(c + 1) * A_DK]
            lhs = jnp.concatenate([blk[:, A_DV:].astype(BF16), qeff[j][c * CHUNK:(c + 1) * CHUNK]],
                                  axis=0)
            res = _dot(lhs, st[j].astype(BF16))
            outs[j].append(res[A_DK:])
            ld = jnp.exp(gcc_ref[j, c * CHUNK + CHUNK - 1:(c + 1) * CHUNK, :])
            st[j] = st[j] * jnp.concatenate([ld, ld], axis=1) + blk[:, :A_DV] - res[:A_DK]

    na = na_ref[...]
    for j in heads:
        state_ref[j] = st[j]
        o = jnp.concatenate(outs[j], axis=0) + aw[j][:, :A_DV]
        ms = jnp.mean(o * o, axis=-1, keepdims=True)
        z = z_ref[:, j * A_DV:(j + 1) * A_DV].astype(F32)
        o_ref[:, j * A_DV:(j + 1) * A_DV] = (
            o * lax.rsqrt(ms + EPS) * na * _silu(z)).astype(o_ref.dtype)


def _gdn_kernel(q_ref, k_ref, v_ref, z_ref, st_ref, alog_ref, dtb_ref, cq_ref, ck_ref, cv_ref,
                na_ref, o_ref, eq_ref, ek_ref, ev_ref, state_ref, bmask_ref, sel_ref, *stage):
    grp = pl.program_id(1)
    r = pl.program_id(2)
    stage_a, stage_b = stage[:len(stage) // 2], stage[len(stage) // 2:]

    @pl.when((pl.program_id(0) == 0) & (grp == 0) & (r == 0))
    def _():
        row = lax.broadcasted_iota(jnp.int32, (SB, SB), 0)
        col = lax.broadcasted_iota(jnp.int32, (SB, SB), 1)

        def same(bits):
            return ((row >> bits) == (col >> bits)).astype(F32)

        m16, m32, m64 = same(4), same(5), same(6)
        lower = (row >= col).astype(F32)
        bmask_ref[_B_CAUSAL] = (m64 * lower).astype(BF16)
        bmask_ref[_B_NEG16] = (-(m16 * (row > col).astype(F32))).astype(BF16)
        bmask_ref[_B_C1] = ((m32 - m16) * lower).astype(BF16)
        bmask_ref[_B_C2] = ((m64 - m32) * lower).astype(BF16)
        bmask_ref[_B_EYE] = (row == col).astype(F32).astype(BF16)
        rsel = lax.broadcasted_iota(jnp.int32, (CPS * A_DK, SB), 0)
        csel = lax.broadcasted_iota(jnp.int32, (CPS * A_DK, SB), 1)
        sel_ref[...] = ((rsel >> 7) == (csel >> 6)).astype(F32).astype(BF16)
        for ref in stage_b:
            ref[...] = jnp.zeros_like(ref)
        state_ref[...] = jnp.zeros_like(state_ref)

    @pl.when(r == 0)
    def _():
        eq_ref[...] = jnp.zeros_like(eq_ref)
        ek_ref[...] = jnp.zeros_like(ek_ref)
        ev_ref[...] = jnp.zeros_like(ev_ref)

    def step(dst, src):
        _gdn_prep(grp, q_ref, k_ref, v_ref, st_ref, alog_ref, dtb_ref, cq_ref, ck_ref, cv_ref,
                  eq_ref, ek_ref, ev_ref, dst)
        _gdn_main(r <= 1, src, z_ref, na_ref, o_ref, state_ref, bmask_ref, sel_ref)

    @pl.when(r % 2 == 0)
    def _():
        step(stage_a, stage_b)

    @pl.when(r % 2 == 1)
    def _():
        step(stage_b, stage_a)


def _gdn(proj, small_t, alog_b, dtb_b, conv_a, norm_a, batch, seq):
    t = proj.shape[0]
    nr = seq // SB
    wq, wv = HG * A_DK, HG * A_DV
    k_off = A_QK // wq
    v_off = (2 * A_QK) // wv
    z_off = (2 * A_QK + A_V) // wv

    def staged(b, g, r):
        return b * nr + jnp.minimum(r, nr - 1)

    def solved(b, g, r):
        return b * nr + jnp.maximum(r - 1, 0)

    stage = [
        pltpu.VMEM((HG, 2 * SB, A_DK), BF16),
        pltpu.VMEM((HG, SB, A_DK), BF16),
        pltpu.VMEM((HG, SB, A_DV + A_DK), BF16),
        pltpu.VMEM((HG, SB, A_DK), BF16),
        pltpu.VMEM((HG, A_DK, SB), BF16),
        pltpu.VMEM((HG, SB, LANES), F32),
        pltpu.VMEM((HG, 8, SB), F32),
    ]
    return pl.pallas_call(
        _gdn_kernel,
        grid=(batch, A_HEADS // HG, nr + 1),
        in_specs=[
            pl.BlockSpec((SB, wq), lambda b, g, r: (staged(b, g, r), g)),
            pl.BlockSpec((SB, wq), lambda b, g, r: (staged(b, g, r), k_off + g)),
            pl.BlockSpec((SB, wv), lambda b, g, r: (staged(b, g, r), v_off + g)),
            pl.BlockSpec((SB, wv), lambda b, g, r: (solved(b, g, r), z_off + g)),
            pl.BlockSpec((None, 2 * A_HEADS, SB), lambda b, g, r: (b, 0, jnp.minimum(r, nr - 1))),
            pl.BlockSpec((A_HEADS, SB), lambda b, g, r: (0, 0)),
            pl.BlockSpec((A_HEADS, SB), lambda b, g, r: (0, 0)),
            pl.BlockSpec((CONV_W, wq), lambda b, g, r: (0, g)),
            pl.BlockSpec((CONV_W, wq), lambda b, g, r: (0, k_off + g)),
            pl.BlockSpec((CONV_W, wv), lambda b, g, r: (0, v_off + g)),
            pl.BlockSpec((1, A_DV), lambda b, g, r: (0, 0)),
        ],
        out_specs=pl.BlockSpec((SB, wv), lambda b, g, r: (solved(b, g, r), g)),
        out_shape=jax.ShapeDtypeStruct((t, A_V), BF16),
        scratch_shapes=[
            pltpu.VMEM((8, wq), F32),
            pltpu.VMEM((8, wq), F32),
            pltpu.VMEM((8, wv), F32),
            pltpu.VMEM((HG, A_DK, A_DV), F32),
            pltpu.VMEM((_N_BMASKS, SB, SB), BF16),
            pltpu.VMEM((CPS * A_DK, SB), BF16),
        ] + stage + stage,
        compiler_params=pltpu.CompilerParams(
            dimension_semantics=("arbitrary", "arbitrary", "arbitrary"),
            vmem_limit_bytes=VMEM_LIMIT),
        name="gdn",
    )(proj, proj, proj, proj, small_t, alog_b, dtb_b, conv_a, conv_a, conv_a, norm_a)


_D_MASK, _D_Q, _D_K, _D_SB, _N_DEC = 0, 1, 2, 3, 4


def _ret_kernel(q_ref, k_ref, v_ref, g_ref, nb_ref, o_ref, state_ref, dec_ref):
    h = pl.program_id(1)
    r = pl.program_id(2)
    lb = q_ref.shape[0]
    nsb = lb // SB

    @pl.when(r == 0)
    def _():
        state_ref[...] = jnp.zeros_like(state_ref)
        hf = jnp.full((SB, SB), h, jnp.int32).astype(F32)
        lg = jnp.log1p(-jnp.exp2(-5.0 - hf))
        row = lax.broadcasted_iota(jnp.int32, (SB, SB), 0)
        col = lax.broadcasted_iota(jnp.int32, (SB, SB), 1)
        allow = ((col >> 6) <= (row >> 6)).astype(F32)
        dist = jnp.abs(row - col).astype(F32)
        rowf = row.astype(F32)
        dec_ref[_D_MASK] = jnp.exp(lg * dist) * allow * (B_DK ** -0.5)
        dec_ref[_D_Q] = jnp.exp(lg * (rowf + 1.0))
        dec_ref[_D_K] = jnp.exp(lg * (SB - 1.0 - rowf)) * (B_DK ** -0.5)
        dec_ref[_D_SB] = jnp.exp(lg * SB)

    mask = dec_ref[_D_MASK]
    q_dec = dec_ref[_D_Q]
    k_dec = dec_ref[_D_K]
    sb_dec = dec_ref[_D_SB][0:1, 0:1]
    nb = nb_ref[...]

    state = state_ref[...]
    for s in range(nsb):
        r0 = s * SB
        qb = q_ref[pl.ds(r0, SB), :]
        kb = k_ref[pl.ds(r0, SB), :]
        vb = v_ref[pl.ds(r0, SB), :]
        sc = lax.dot_general(qb, kb, _NT, preferred_element_type=F32)
        o = (_dot((sc * mask).astype(BF16), vb)
             + _dot((qb.astype(F32) * q_dec).astype(BF16), state.astype(BF16)))
        state = state * sb_dec + lax.dot_general(
            (kb.astype(F32) * k_dec).astype(BF16), vb, _TN, preferred_element_type=F32)
        ms = jnp.mean(o * o, axis=-1, keepdims=True)
        gate = g_ref[pl.ds(r0, SB), :].astype(F32)
        o_ref[pl.ds(r0, SB), :] = (o * lax.rsqrt(ms + EPS) * nb * gate).astype(o_ref.dtype)
    state_ref[...] = state


def _ret(proj, norm_b, batch, seq, lb):
    t = proj.shape[0]
    nr = seq // lb
    base = 2 * A_QK + 2 * A_V
    q_off = base // B_DK
    k_off = (base + B_QK) // B_DK
    v_off = (base + 2 * B_QK) // B_DV
    g_off = (base + 2 * B_QK + B_V) // B_DV

    def rows(b, h, r):
        return b * nr + r

    return pl.pallas_call(
        _ret_kernel,
        grid=(batch, B_HEADS, nr),
        in_specs=[
            pl.BlockSpec((lb, B_DK), lambda b, h, r: (rows(b, h, r), q_off + h)),
            pl.BlockSpec((lb, B_DK), lambda b, h, r: (rows(b, h, r), k_off + h)),
            pl.BlockSpec((lb, B_DV), lambda b, h, r: (rows(b, h, r), v_off + h)),
            pl.BlockSpec((lb, B_DV), lambda b, h, r: (rows(b, h, r), g_off + h)),
            pl.BlockSpec((1, B_DV), lambda b, h, r: (0, h)),
        ],
        out_specs=pl.BlockSpec((lb, B_DV), lambda b, h, r: (rows(b, h, r), h)),
        out_shape=jax.ShapeDtypeStruct((t, B_V), BF16),
        scratch_shapes=[pltpu.VMEM((B_DK, B_DV), F32), pltpu.VMEM((_N_DEC, SB, SB), F32)],
        compiler_params=pltpu.CompilerParams(
            dimension_semantics=("arbitrary", "arbitrary", "arbitrary"),
            vmem_limit_bytes=VMEM_LIMIT),
        name="ret",
    )(proj, proj, proj, proj, norm_b)


def _merge_kernel(oa_ref, ob_ref, ga_ref, gb_ref, x_ref, wa_ref, wb_ref, wo_ref, n_ref, o_ref):
    ya = _dot(oa_ref[...], wa_ref[...])
    yb = _dot(ob_ref[...], wb_ref[...])
    ga = jax.nn.sigmoid(ga_ref[...].astype(F32))
    gb = jax.nn.sigmoid(gb_ref[...].astype(F32))
    mix = _dot((ga * ya + gb * yb).astype(BF16), wo_ref[...])
    ms = jnp.mean(mix * mix, axis=-1, keepdims=True)
    o_ref[...] = x_ref[...] + mix * lax.rsqrt(ms + EPS) * n_ref[...]


def _merge(oa, ob, proj, x2, wa, wb, wo, gain, tm):
    t, d = x2.shape
    gate_off = (2 * A_QK + 2 * A_V + 2 * B_QK + 2 * B_V) // d
    const = lambda i: (0, 0)
    return pl.pallas_call(
        _merge_kernel,
        grid=(t // tm,),
        in_specs=[
            pl.BlockSpec((tm, A_V), lambda i: (i, 0)),
            pl.BlockSpec((tm, B_V), lambda i: (i, 0)),
            pl.BlockSpec((tm, d), lambda i: (i, gate_off)),
            pl.BlockSpec((tm, d), lambda i: (i, gate_off + 1)),
            pl.BlockSpec((tm, d), lambda i: (i, 0)),
            pl.BlockSpec((A_V, d), const),
            pl.BlockSpec((B_V, d), const),
            pl.BlockSpec((d, d), const),
            pl.BlockSpec((1, d), const),
        ],
        out_specs=pl.BlockSpec((tm, d), lambda i: (i, 0)),
        out_shape=jax.ShapeDtypeStruct((t, d), F32),
        compiler_params=pltpu.CompilerParams(
            dimension_semantics=("arbitrary",), vmem_limit_bytes=VMEM_LIMIT),
        name="merge",
    )(oa, ob, proj, proj, x2, wa, wb, wo, gain)


def _mlp_kernel(x_ref, gpre_ref, wu_ref, wd_ref, gpost_ref, o_ref, *, ff_tile):
    x = x_ref[...]
    ms = jnp.mean(x * x, axis=-1, keepdims=True)
    hn = (x * lax.rsqrt(ms + EPS) * gpre_ref[...]).astype(BF16)
    d_ff = wu_ref.shape[1]
    ff = jnp.zeros(x.shape, F32)
    for j in range(d_ff // ff_tile):
        up = _dot(hn, wu_ref[:, j * ff_tile:(j + 1) * ff_tile])
        act = jnp.square(jnp.maximum(up, 0.0)).astype(BF16)
        ff = ff + _dot(act, wd_ref[j * ff_tile:(j + 1) * ff_tile, :])
    ms2 = jnp.mean(ff * ff, axis=-1, keepdims=True)
    o_ref[...] = x + ff * lax.rsqrt(ms2 + EPS) * gpost_ref[...]


def _mlp(x1, gpre, wu, wd, gpost, tm, ff_tile):
    t, d = x1.shape
    d_ff = wu.shape[1]
    const = lambda i: (0, 0)
    return pl.pallas_call(
        functools.partial(_mlp_kernel, ff_tile=ff_tile),
        grid=(t // tm,),
        in_specs=[
            pl.BlockSpec((tm, d), lambda i: (i, 0)),
            pl.BlockSpec((1, d), const),
            pl.BlockSpec((d, d_ff), const),
            pl.BlockSpec((d_ff, d), const),
            pl.BlockSpec((1, d), const),
        ],
        out_specs=pl.BlockSpec((tm, d), lambda i: (i, 0)),
        out_shape=jax.ShapeDtypeStruct((t, d), F32),
        compiler_params=pltpu.CompilerParams(
            dimension_semantics=("arbitrary",), vmem_limit_bytes=VMEM_LIMIT),
        name="mlp",
    )(x1, gpre, wu, wd, gpost)


def _pick(n, candidates):
    for c in candidates:
        if n % c == 0:
            return c
    raise ValueError(f"no tile for {n} among {candidates}")


def _layer(x, n_mix_pre, n_mix_post, n_mlp_pre, n_mlp_post, w_in, conv_a, a_log, dt_bias,
           norm_a, norm_b, w_br_a, w_br_b, w_out, w_up, w_down):
    batch, seq, d = x.shape
    t = batch * seq
    assert seq % SB == 0
    x2 = x.reshape(t, d)

    n_small_lo = 2 * A_QK + 2 * A_V
    n_small_hi = n_small_lo + 2 * A_HEADS
    w16 = w_in.astype(BF16)
    w_right = w16[:, n_small_hi:]
    w_small = jnp.pad(w16[:, n_small_lo:n_small_hi], ((0, 0), (0, LANES - 2 * A_HEADS)))

    tm = _pick(seq, (1024, 512, 256))
    inv_freq = ROPE_BASE ** (-jnp.arange(0, B_DK, 2, dtype=F32) / B_DK)
    ang_lo = jnp.arange(tm, dtype=F32)[:, None] * inv_freq[None, :]
    ang_hi = (jnp.arange(seq // tm, dtype=F32) * tm)[:, None] * inv_freq[None, :]
    tables = (jnp.cos(ang_lo), jnp.sin(ang_lo), jnp.cos(ang_hi), jnp.sin(ang_hi))

    proj, small = _inproj(x2, n_mix_pre.reshape(1, d), w16, w_right, w_small, tables, tm, seq)

    small_t = jnp.transpose(small[:, :2 * A_HEADS].reshape(batch, seq, 2 * A_HEADS), (0, 2, 1))
    alog_b = jnp.broadcast_to(a_log.reshape(A_HEADS, 1), (A_HEADS, SB))
    dtb_b = jnp.broadcast_to(dt_bias.reshape(A_HEADS, 1), (A_HEADS, SB))
    oa = _gdn(proj, small_t, alog_b, dtb_b, conv_a, norm_a.reshape(1, A_DV), batch, seq)

    lb = _pick(seq, (2048, 1024, 512, 256))
    ob = _ret(proj, norm_b.reshape(1, B_V), batch, seq, lb)

    tm2 = _pick(t, (512, 256))
    x1 = _merge(oa, ob, proj, x2, w_br_a.astype(BF16), w_br_b.astype(BF16), w_out.astype(BF16),
                n_mix_post.reshape(1, d), tm2)
    out = _mlp(x1, n_mlp_pre.reshape(1, d), w_up.astype(BF16), w_down.astype(BF16),
               n_mlp_post.reshape(1, d), tm2, 1024)
    return out.reshape(batch, seq, d)


def kernel(x, norm_mix_pre, norm_mix_post, norm_mlp_pre, norm_mlp_post, w_in, conv_a, a_log,
           dt_bias, norm_a, norm_b, w_br_a, w_br_b, w_out, w_up, w_down):
    for l in range(w_in.shape[0]):
        x = _layer(x, norm_mix_pre[l], norm_mix_post[l], norm_mlp_pre[l], norm_mlp_post[l],
                   w_in[l], conv_a[l], a_log[l], dt_bias[l], norm_a[l], norm_b[l],
                   w_br_a[l], w_br_b[l], w_out[l], w_up[l], w_down[l])
    return x
```

```python
import functools

import jax
import jax.numpy as jnp
from jax import lax
from jax.experimental import pallas as pl
from jax.experimental.pallas import tpu as pltpu

F32 = jnp.float32
BF16 = jnp.bfloat16

CHUNK = 64
EPS = 1e-6
A_HEADS, A_DK, A_DV, CONV_W = 8, 128, 256, 4
B_HEADS, B_DK, B_DV = 4, 256, 512
ROPE_BASE = 10000.0
A_QK, A_V = A_HEADS * A_DK, A_HEADS * A_DV
B_QK, B_V = B_HEADS * B_DK, B_HEADS * B_DV

SB = 256
CPS = SB // CHUNK
HB = SB // 2
LANES = 128
HG = 8
VMEM_LIMIT = 56 * 1024 * 1024

_PW = 2048
_T_SPLIT = (2 * A_QK + 2 * A_V) // _PW
_T_ROT = _T_SPLIT
_T_GB = (2 * A_QK + 2 * A_V + 2 * B_QK + B_V) // _PW

_NT = (((1,), (1,)), ((), ()))
_TN = (((0,), (0,)), ((), ()))


def _dot(a, b):
    return jnp.dot(a, b, preferred_element_type=F32)


def _silu(x):
    h = 0.5 * x
    return h * (1.0 + jnp.tanh(h))


def _inproj_kernel(x_ref, g_ref, wl_ref, wr_ref, ws_ref, cl_ref, sl_ref, ch_ref, sh_ref,
                   o_ref, os_ref, xn_ref, *, tiles_per_seq):
    i = pl.program_id(0)
    j = pl.program_id(1)

    @pl.when(j == 0)
    def _():
        x = x_ref[...]
        ms = jnp.mean(x * x, axis=-1, keepdims=True)
        xn = (x * lax.rsqrt(ms + EPS) * g_ref[...]).astype(BF16)
        xn_ref[...] = xn
        os_ref[...] = _dot(xn, ws_ref[...])

    @pl.when(j < _T_SPLIT)
    def _():
        o_ref[...] = _dot(xn_ref[...], wl_ref[...]).astype(o_ref.dtype)

    @pl.when(j == _T_ROT)
    def _():
        acc = _dot(xn_ref[...], wr_ref[...])
        k = i % tiles_per_seq
        ch = ch_ref[pl.ds(k, 1), :]
        sh = sh_ref[pl.ds(k, 1), :]
        cos = ch * cl_ref[...] - sh * sl_ref[...]
        sin = sh * cl_ref[...] + ch * sl_ref[...]
        half = B_DK // 2
        for hd in range(_PW // B_DK):
            x1 = acc[:, hd * B_DK:hd * B_DK + half]
            x2 = acc[:, hd * B_DK + half:(hd + 1) * B_DK]
            o_ref[:, hd * B_DK:hd * B_DK + half] = (x1 * cos - x2 * sin).astype(o_ref.dtype)
            o_ref[:, hd * B_DK + half:(hd + 1) * B_DK] = (x1 * sin + x2 * cos).astype(o_ref.dtype)

    @pl.when(j == _T_GB)
    def _():
        o_ref[...] = _silu(_dot(xn_ref[...], wr_ref[...])).astype(o_ref.dtype)

    @pl.when((j >= _T_SPLIT) & (j != _T_ROT) & (j != _T_GB))
    def _():
        o_ref[...] = _dot(xn_ref[...], wr_ref[...]).astype(o_ref.dtype)


def _inproj(x2, gain, w_left, w_right, w_small, tables, tm, seq):
    t, d = x2.shape
    n_right = w_right.shape[1] // _PW
    n = (_T_SPLIT + n_right) * _PW
    assert w_left.shape[1] >= _T_SPLIT * _PW and w_right.shape[1] % _PW == 0 and seq % tm == 0
    tiles_per_seq = seq // tm
    cos_lo, sin_lo, cos_hi, sin_hi = tables
    const = lambda i, j: (0, 0)
    return pl.pallas_call(
        functools.partial(_inproj_kernel, tiles_per_seq=tiles_per_seq),
        grid=(t // tm, n // _PW),
        in_specs=[
            pl.BlockSpec((tm, d), lambda i, j: (i, 0)),
            pl.BlockSpec((1, d), const),
            pl.BlockSpec((d, _PW), lambda i, j: (0, jnp.minimum(j, _T_SPLIT - 1))),
            pl.BlockSpec((d, _PW), lambda i, j: (0, jnp.where(j < _T_SPLIT, n_right - 1, j - _T_SPLIT))),
            pl.BlockSpec((d, LANES), const),
            pl.BlockSpec((tm, B_DK // 2), const),
            pl.BlockSpec((tm, B_DK // 2), const),
            pl.BlockSpec((tiles_per_seq, B_DK // 2), const),
            pl.BlockSpec((tiles_per_seq, B_DK // 2), const),
        ],
        out_specs=[
            pl.BlockSpec((tm, _PW), lambda i, j: (i, j)),
            pl.BlockSpec((tm, LANES), lambda i, j: (i, 0)),
        ],
        out_shape=[
            jax.ShapeDtypeStruct((t, n), BF16),
            jax.ShapeDtypeStruct((t, LANES), F32),
        ],
        scratch_shapes=[pltpu.VMEM((tm, d), BF16)],
        compiler_params=pltpu.CompilerParams(
            dimension_semantics=("arbitrary", "arbitrary"), vmem_limit_bytes=VMEM_LIMIT),
        name="inproj",
    )(x2, gain, w_left, w_right, w_small, cos_lo, sin_lo, cos_hi, sin_hi)


def _conv_silu(tail_ref, x_ref, c_ref):
    x = x_ref[...].astype(F32)
    rows = x.shape[0]
    xe = jnp.concatenate([tail_ref[...], x], axis=0)
    tail_ref[...] = x[rows - 8:, :]
    c_half = 0.5 * c_ref[...]
    h = x * c_half[CONV_W - 1:CONV_W, :]
    for d in range(1, CONV_W):
        h = h + pltpu.roll(xe, d, axis=0)[8:, :] * c_half[CONV_W - 1 - d:CONV_W - d, :]
    return h * (1.0 + jnp.tanh(h))


def _l2norm(y):
    return y * lax.rsqrt(jnp.sum(y * y, axis=-1, keepdims=True) + EPS)


def _group_cumsum_rows(x, group):
    lane = lax.broadcasted_iota(jnp.int32, x.shape, 1)
    pos = lane & (group - 1)
    s = 1
    while s < group:
        shifted = pltpu.roll(x, s, axis=1)
        x = x + jnp.where(pos >= s, shifted, 0.0)
        s *= 2
    return x


_B_NEG16, _B_C1, _B_C2, _B_EYE, _B_CAUSAL, _N_BMASKS = 0, 1, 2, 3, 4, 5


def _col_replicated(row):
    return jnp.transpose(jnp.broadcast_to(row, (LANES, SB)))


def _gdn_prep(grp, q_ref, k_ref, v_ref, st_ref, alog_ref, dtb_ref, cq_ref, ck_ref, cv_ref,
              eq_ref, ek_ref, ev_ref, dst):
    lq_ref, kk_ref, rhs_ref, qe_ref, kdt_ref, gcc_ref, grs_ref = dst
    qa = _conv_silu(eq_ref, q_ref, cq_ref)
    ka = _conv_silu(ek_ref, k_ref, ck_ref)
    va = _conv_silu(ev_ref, v_ref, cv_ref)
    for j in range(HG):
        hh = grp * HG + j
        beta_row = jax.nn.sigmoid(st_ref[pl.ds(hh, 1), :])
        xg = st_ref[pl.ds(A_HEADS + hh, 1), :] + dtb_ref[pl.ds(hh, 1), :]
        softplus = jnp.maximum(xg, 0.0) + jnp.log1p(jnp.exp(-jnp.abs(xg)))
        g_row = -jnp.exp(alog_ref[pl.ds(hh, 1), :]) * softplus
        gc_row8 = _group_cumsum_rows(jnp.broadcast_to(g_row, (8, SB)), CHUNK)
        gc = _col_replicated(gc_row8[0:1])
        bc = _col_replicated(beta_row)
        grs_ref[j] = gc_row8
        gcc_ref[j] = gc

        q = _l2norm(qa[:, j * A_DK:(j + 1) * A_DK]) * (A_DK ** -0.5)
        k = _l2norm(ka[:, j * A_DK:(j + 1) * A_DK])
        v = va[:, j * A_DV:(j + 1) * A_DV]
        eg = jnp.exp(gc)
        kb = k * bc
        glast_rep = jnp.concatenate(
            [jnp.broadcast_to(gc[c * CHUNK + CHUNK - 1:(c + 1) * CHUNK, :], (CHUNK, LANES))
             for c in range(CPS)], axis=0)
        lq_ref[j] = jnp.concatenate([kb, q], axis=0).astype(BF16)
        kk_ref[j] = k.astype(BF16)
        rhs_ref[j] = jnp.concatenate([v * jnp.concatenate([bc, bc], axis=1), kb * eg],
                                     axis=1).astype(BF16)
        qe_ref[j] = (q * eg).astype(BF16)
        kdt_ref[j] = jnp.transpose(k * jnp.exp(glast_rep - gc)).astype(BF16)


def _gdn_main(fresh, src, z_ref, na_ref, o_ref, state_ref, bmask_ref, sel_ref):
    lq_ref, kk_ref, rhs_ref, qe_ref, kdt_ref, gcc_ref, grs_ref = src
    heads = range(HG)

    kq = [lax.dot_general(lq_ref[j], kk_ref[j], _NT, preferred_element_type=F32)
          for j in heads]

    def full(bl):
        z = jnp.zeros_like(bl[0])
        return jnp.concatenate([jnp.concatenate([bl[0], z], axis=1),
                                jnp.concatenate([z, bl[1]], axis=1)], axis=0)

    def diag(x):
        r = x.shape[0] // 2
        return [x[:r, :HB], x[r:, HB:]]

    def bdot(a, b):
        return [m.astype(BF16) for m in diag(_dot(full(a), full(b)))]

    def masked_scores(j):
        gc, gr = gcc_ref[j], grs_ref[j][0:1]
        out = []
        for b in range(2):
            rs = slice(b * HB, (b + 1) * HB)
            decay = jnp.exp(jnp.minimum(gc[rs] - gr[:, rs], 0.0))
            ab = (kq[j][rs, rs] * decay).astype(BF16)
            n1 = ab * bmask_ref[_B_NEG16]
            qk = kq[j][SB + b * HB:SB + (b + 1) * HB, rs]
            out.append(((qk * decay).astype(BF16) * bmask_ref[_B_CAUSAL], n1,
                        bmask_ref[_B_EYE] + n1, ab * bmask_ref[_B_C1], ab * bmask_ref[_B_C2]))
        return [list(t) for t in zip(*out)]

    attn, n1b, pb, c1b, c2b = zip(*[masked_scores(j) for j in heads])
    eye = bmask_ref[_B_EYE]

    n2b = [bdot(n1b[j], n1b[j]) for j in heads]
    pb = [bdot(pb[j], [eye + m for m in n2b[j]]) for j in heads]
    n4b = [bdot(n2b[j], n2b[j]) for j in heads]
    pb = [bdot(pb[j], [eye + m for m in n4b[j]]) for j in heads]
    n8b = [bdot(n4b[j], n4b[j]) for j in heads]
    pb = [bdot(pb[j], [eye + m for m in n8b[j]]) for j in heads]
    tinv = pb
    for blk, cb in ((16, c1b), (32, c2b)):
        low = lambda x: jnp.concatenate(
            [x[b + blk:b + 2 * blk] for b in range(0, HB, 2 * blk)], axis=0)
        eye_low = low(eye)
        xm = [[eye_low - m for m in bdot([low(t) for t in tinv[j]], cb[j])] for j in heads]
        t_low = [bdot(xm[j], tinv[j]) for j in heads]
        tinv = [[jnp.concatenate(
            [piece for i, b in enumerate(range(0, HB, 2 * blk))
             for piece in (tinv[j][h][b:b + blk], t_low[j][h][i * blk:(i + 1) * blk])], axis=0)
            for h in range(2)] for j in heads]
    tinv = [full(tinv[j]) for j in heads]
    attn = [full(attn[j]) for j in heads]

    uwb = [_dot(tinv[j], rhs_ref[j]).astype(BF16) for j in heads]
    aw = [_dot(attn[j], uwb[j]) for j in heads]
    qeff = [(qe_ref[j].astype(F32) - aw[j][:, A_DV:]).astype(BF16) for j in heads]
    bp = [_dot(jnp.concatenate([kdt_ref[j]] * CPS, axis=0) * sel_ref[...], uwb[j])
          for j in heads]

    st = [jnp.where(fresh, 0.0, state_ref[j]) for j in heads]
    outs = [[] for _ in heads]
    for c in range(CPS):
        for j in heads:
            blk = bp[j][c * A_DK:(c + 1) * A_DK]
            lhs = jnp.concatenate([blk[:, A_DV:].astype(BF16), qeff[j][c * CHUNK:(c + 1) * CHUNK]],
                                  axis=0)
            res = _dot(lhs, st[j].astype(BF16))
            outs[j].append(res[A_DK:])
            ld = jnp.exp(gcc_ref[j, c * CHUNK + CHUNK - 1:(c + 1) * CHUNK, :])
            st[j] = st[j] * jnp.concatenate([ld, ld], axis=1) + blk[:, :A_DV] - res[:A_DK]

    na = na_ref[...]
    for j in heads:
        state_ref[j] = st[j]
        o = jnp.concatenate(outs[j], axis=0) + aw[j][:, :A_DV]
        ms = jnp.mean(o * o, axis=-1, keepdims=True)
        z = z_ref[:, j * A_DV:(j + 1) * A_DV].astype(F32)
        o_ref[:, j * A_DV:(j + 1) * A_DV] = (
            o * lax.rsqrt(ms + EPS) * na * _silu(z)).astype(o_ref.dtype)


def _gdn_kernel(q_ref, k_ref, v_ref, z_ref, st_ref, alog_ref, dtb_ref, cq_ref, ck_ref, cv_ref,
                na_ref, o_ref, eq_ref, ek_ref, ev_ref, state_ref, bmask_ref, sel_ref, *stage):
    grp = pl.program_id(1)
    r = pl.program_id(2)
    stage_a, stage_b = stage[:len(stage) // 2], stage[len(stage) // 2:]

    @pl.when((pl.program_id(0) == 0) & (grp == 0) & (r == 0))
    def _():
        row = lax.broadcasted_iota(jnp.int32, (HB, HB), 0)
        col = lax.broadcasted_iota(jnp.int32, (HB, HB), 1)

        def same(bits):
            return ((row >> bits) == (col >> bits)).astype(F32)

        m16, m32, m64 = same(4), same(5), same(6)
        lower = (row >= col).astype(F32)
        bmask_ref[_B_CAUSAL] = (m64 * lower).astype(BF16)
        bmask_ref[_B_NEG16] = (-(m16 * (row > col).astype(F32))).astype(BF16)
        bmask_ref[_B_C1] = ((m32 - m16) * lower).astype(BF16)
        bmask_ref[_B_C2] = ((m64 - m32) * lower).astype(BF16)
        bmask_ref[_B_EYE] = (row == col).astype(F32).astype(BF16)
        rsel = lax.broadcasted_iota(jnp.int32, (CPS * A_DK, SB), 0)
        csel = lax.broadcasted_iota(jnp.int32, (CPS * A_DK, SB), 1)
        sel_ref[...] = ((rsel >> 7) == (csel >> 6)).astype(F32).astype(BF16)
        for ref in stage_b:
            ref[...] = jnp.zeros_like(ref)
        state_ref[...] = jnp.zeros_like(state_ref)

    @pl.when(r == 0)
    def _():
        eq_ref[...] = jnp.zeros_like(eq_ref)
        ek_ref[...] = jnp.zeros_like(ek_ref)
        ev_ref[...] = jnp.zeros_like(ev_ref)

    def step(dst, src):
        _gdn_prep(grp, q_ref, k_ref, v_ref, st_ref, alog_ref, dtb_ref, cq_ref, ck_ref, cv_ref,
                  eq_ref, ek_ref, ev_ref, dst)
        _gdn_main(r <= 1, src, z_ref, na_ref, o_ref, state_ref, bmask_ref, sel_ref)

    @pl.when(r % 2 == 0)
    def _():
        step(stage_a, stage_b)

    @pl.when(r % 2 == 1)
    def _():
        step(stage_b, stage_a)


def _gdn(proj, small_t, alog_b, dtb_b, conv_a, norm_a, batch, seq):
    t = proj.shape[0]
    nr = seq // SB
    wq, wv = HG * A_DK, HG * A_DV
    k_off = A_QK // wq
    v_off = (2 * A_QK) // wv
    z_off = (2 * A_QK + A_V) // wv

    def staged(b, g, r):
        return b * nr + jnp.minimum(r, nr - 1)

    def solved(b, g, r):
        return b * nr + jnp.maximum(r - 1, 0)

    stage = [
        pltpu.VMEM((HG, 2 * SB, A_DK), BF16),
        pltpu.VMEM((HG, SB, A_DK), BF16),
        pltpu.VMEM((HG, SB, A_DV + A_DK), BF16),
        pltpu.VMEM((HG, SB, A_DK), BF16),
        pltpu.VMEM((HG, A_DK, SB), BF16),
        pltpu.VMEM((HG, SB, LANES), F32),
        pltpu.VMEM((HG, 8, SB), F32),
    ]
    return pl.pallas_call(
        _gdn_kernel,
        grid=(batch, A_HEADS // HG, nr + 1),
        in_specs=[
            pl.BlockSpec((SB, wq), lambda b, g, r: (staged(b, g, r), g)),
            pl.BlockSpec((SB, wq), lambda b, g, r: (staged(b, g, r), k_off + g)),
            pl.BlockSpec((SB, wv), lambda b, g, r: (staged(b, g, r), v_off + g)),
            pl.BlockSpec((SB, wv), lambda b, g, r: (solved(b, g, r), z_off + g)),
            pl.BlockSpec((None, 2 * A_HEADS, SB), lambda b, g, r: (b, 0, jnp.minimum(r, nr - 1))),
            pl.BlockSpec((A_HEADS, SB), lambda b, g, r: (0, 0)),
            pl.BlockSpec((A_HEADS, SB), lambda b, g, r: (0, 0)),
            pl.BlockSpec((CONV_W, wq), lambda b, g, r: (0, g)),
            pl.BlockSpec((CONV_W, wq), lambda b, g, r: (0, k_off + g)),
            pl.BlockSpec((CONV_W, wv), lambda b, g, r: (0, v_off + g)),
            pl.BlockSpec((1, A_DV), lambda b, g, r: (0, 0)),
        ],
        out_specs=pl.BlockSpec((SB, wv), lambda b, g, r: (solved(b, g, r), g)),
        out_shape=jax.ShapeDtypeStruct((t, A_V), BF16),
        scratch_shapes=[
            pltpu.VMEM((8, wq), F32),
            pltpu.VMEM((8, wq), F32),
            pltpu.VMEM((8, wv), F32),
            pltpu.VMEM((HG, A_DK, A_DV), F32),
            pltpu.VMEM((_N_BMASKS, HB, HB), BF16),
            pltpu.VMEM((CPS * A_DK, SB), BF16),
        ] + stage + stage,
        compiler_params=pltpu.CompilerParams(
            dimension_semantics=("arbitrary", "arbitrary", "arbitrary"),
            vmem_limit_bytes=VMEM_LIMIT),
        name="gdn",
    )(proj, proj, proj, proj, small_t, alog_b, dtb_b, conv_a, conv_a, conv_a, norm_a)


_D_MASK, _D_Q, _D_K, _D_SB, _N_DEC = 0, 1, 2, 3, 4


def _ret_kernel(q_ref, k_ref, v_ref, g_ref, nb_ref, o_ref, state_ref, dec_ref):
    h = pl.program_id(1)
    r = pl.program_id(2)
    lb = q_ref.shape[0]
    nsb = lb // SB

    @pl.when(r == 0)
    def _():
        state_ref[...] = jnp.zeros_like(state_ref)
        hf = jnp.full((SB, SB), h, jnp.int32).astype(F32)
        lg = jnp.log1p(-jnp.exp2(-5.0 - hf))
        row = lax.broadcasted_iota(jnp.int32, (SB, SB), 0)
        col = lax.broadcasted_iota(jnp.int32, (SB, SB), 1)
        allow = ((col >> 6) <= (row >> 6)).astype(F32)
        dist = jnp.abs(row - col).astype(F32)
        rowf = row.astype(F32)
        dec_ref[_D_MASK] = jnp.exp(lg * dist) * allow * (B_DK ** -0.5)
        dec_ref[_D_Q] = jnp.exp(lg * (rowf + 1.0))
        dec_ref[_D_K] = jnp.exp(lg * (SB - 1.0 - rowf)) * (B_DK ** -0.5)
        dec_ref[_D_SB] = jnp.exp(lg * SB)

    mask = dec_ref[_D_MASK]
    q_dec = dec_ref[_D_Q]
    k_dec = dec_ref[_D_K]
    sb_dec = dec_ref[_D_SB][0:1, 0:1]
    nb = nb_ref[...]

    state = state_ref[...]
    for s in range(nsb):
        r0 = s * SB
        qb = q_ref[pl.ds(r0, SB), :]
        kb = k_ref[pl.ds(r0, SB), :]
        vb = v_ref[pl.ds(r0, SB), :]
        sc = lax.dot_general(qb, kb, _NT, preferred_element_type=F32)
        o = (_dot((sc * mask).astype(BF16), vb)
             + _dot((qb.astype(F32) * q_dec).astype(BF16), state.astype(BF16)))
        state = state * sb_dec + lax.dot_general(
            (kb.astype(F32) * k_dec).astype(BF16), vb, _TN, preferred_element_type=F32)
        ms = jnp.mean(o * o, axis=-1, keepdims=True)
        gate = g_ref[pl.ds(r0, SB), :].astype(F32)
        o_ref[pl.ds(r0, SB), :] = (o * lax.rsqrt(ms + EPS) * nb * gate).astype(o_ref.dtype)
    state_ref[...] = state


def _ret(proj, norm_b, batch, seq, lb):
    t = proj.shape[0]
    nr = seq // lb
    base = 2 * A_QK + 2 * A_V
    q_off = base // B_DK
    k_off = (base + B_QK) // B_DK
    v_off = (base + 2 * B_QK) // B_DV
    g_off = (base + 2 * B_QK + B_V) // B_DV

    def rows(b, h, r):
        return b * nr + r

    return pl.pallas_call(
        _ret_kernel,
        grid=(batch, B_HEADS, nr),
        in_specs=[
            pl.BlockSpec((lb, B_DK), lambda b, h, r: (rows(b, h, r), q_off + h)),
            pl.BlockSpec((lb, B_DK), lambda b, h, r: (rows(b, h, r), k_off + h)),
            pl.BlockSpec((lb, B_DV), lambda b, h, r: (rows(b, h, r), v_off + h)),
            pl.BlockSpec((lb, B_DV), lambda b, h, r: (rows(b, h, r), g_off + h)),
            pl.BlockSpec((1, B_DV), lambda b, h, r: (0, h)),
        ],
        out_specs=pl.BlockSpec((lb, B_DV), lambda b, h, r: (rows(b, h, r), h)),
        out_shape=jax.ShapeDtypeStruct((t, B_V), BF16),
        scratch_shapes=[pltpu.VMEM((B_DK, B_DV), F32), pltpu.VMEM((_N_DEC, SB, SB), F32)],
        compiler_params=pltpu.CompilerParams(
            dimension_semantics=("arbitrary", "arbitrary", "arbitrary"),
            vmem_limit_bytes=VMEM_LIMIT),
        name="ret",
    )(proj, proj, proj, proj, norm_b)


def _merge_kernel(oa_ref, ob_ref, ga_ref, gb_ref, x_ref, wa_ref, wb_ref, wo_ref, n_ref, o_ref):
    ya = _dot(oa_ref[...], wa_ref[...])
    yb = _dot(ob_ref[...], wb_ref[...])
    ga = jax.nn.sigmoid(ga_ref[...].astype(F32))
    gb = jax.nn.sigmoid(gb_ref[...].astype(F32))
    mix = _dot((ga * ya + gb * yb).astype(BF16), wo_ref[...])
    ms = jnp.mean(mix * mix, axis=-1, keepdims=True)
    o_ref[...] = x_ref[...] + mix * lax.rsqrt(ms + EPS) * n_ref[...]


def _merge(oa, ob, proj, x2, wa, wb, wo, gain, tm):
    t, d = x2.shape
    gate_off = (2 * A_QK + 2 * A_V + 2 * B_QK + 2 * B_V) // d
    const = lambda i: (0, 0)
    return pl.pallas_call(
        _merge_kernel,
        grid=(t // tm,),
        in_specs=[
            pl.BlockSpec((tm, A_V), lambda i: (i, 0)),
            pl.BlockSpec((tm, B_V), lambda i: (i, 0)),
            pl.BlockSpec((tm, d), lambda i: (i, gate_off)),
            pl.BlockSpec((tm, d), lambda i: (i, gate_off + 1)),
            pl.BlockSpec((tm, d), lambda i: (i, 0)),
            pl.BlockSpec((A_V, d), const),
            pl.BlockSpec((B_V, d), const),
            pl.BlockSpec((d, d), const),
            pl.BlockSpec((1, d), const),
        ],
        out_specs=pl.BlockSpec((tm, d), lambda i: (i, 0)),
        out_shape=jax.ShapeDtypeStruct((t, d), F32),
        compiler_params=pltpu.CompilerParams(
            dimension_semantics=("arbitrary",), vmem_limit_bytes=VMEM_LIMIT),
        name="merge",
    )(oa, ob, proj, proj, x2, wa, wb, wo, gain)


def _mlp_kernel(x_ref, gpre_ref, wu_ref, wd_ref, gpost_ref, o_ref, *, ff_tile):
    x = x_ref[...]
    ms = jnp.mean(x * x, axis=-1, keepdims=True)
    hn = (x * lax.rsqrt(ms + EPS) * gpre_ref[...]).astype(BF16)
    d_ff = wu_ref.shape[1]
    ff = jnp.zeros(x.shape, F32)
    for j in range(d_ff // ff_tile):
        up = _dot(hn, wu_ref[:, j * ff_tile:(j + 1) * ff_tile])
        act = jnp.square(jnp.maximum(up, 0.0)).astype(BF16)
        ff = ff + _dot(act, wd_ref[j * ff_tile:(j + 1) * ff_tile, :])
    ms2 = jnp.mean(ff * ff, axis=-1, keepdims=True)
    o_ref[...] = x + ff * lax.rsqrt(ms2 + EPS) * gpost_ref[...]


def _mlp(x1, gpre, wu, wd, gpost, tm, ff_tile):
    t, d = x1.shape
    d_ff = wu.shape[1]
    const = lambda i: (0, 0)
    return pl.pallas_call(
        functools.partial(_mlp_kernel, ff_tile=ff_tile),
        grid=(t // tm,),
        in_specs=[
            pl.BlockSpec((tm, d), lambda i: (i, 0)),
            pl.BlockSpec((1, d), const),
            pl.BlockSpec((d, d_ff), const),
            pl.BlockSpec((d_ff, d), const),
            pl.BlockSpec((1, d), const),
        ],
        out_specs=pl.BlockSpec((tm, d), lambda i: (i, 0)),
        out_shape=jax.ShapeDtypeStruct((t, d), F32),
        compiler_params=pltpu.CompilerParams(
            dimension_semantics=("arbitrary",), vmem_limit_bytes=VMEM_LIMIT),
        name="mlp",
    )(x1, gpre, wu, wd, gpost)


def _pick(n, candidates):
    for c in candidates:
        if n % c == 0:
            return c
    raise ValueError(f"no tile for {n} among {candidates}")


def _layer(x, n_mix_pre, n_mix_post, n_mlp_pre, n_mlp_post, w_in, conv_a, a_log, dt_bias,
           norm_a, norm_b, w_br_a, w_br_b, w_out, w_up, w_down):
    batch, seq, d = x.shape
    t = batch * seq
    assert seq % SB == 0
    x2 = x.reshape(t, d)

    n_small_lo = 2 * A_QK + 2 * A_V
    n_small_hi = n_small_lo + 2 * A_HEADS
    w16 = w_in.astype(BF16)
    w_right = w16[:, n_small_hi:]
    w_small = jnp.pad(w16[:, n_small_lo:n_small_hi], ((0, 0), (0, LANES - 2 * A_HEADS)))

    tm = _pick(seq, (1024, 512, 256))
    inv_freq = ROPE_BASE ** (-jnp.arange(0, B_DK, 2, dtype=F32) / B_DK)
    ang_lo = jnp.arange(tm, dtype=F32)[:, None] * inv_freq[None, :]
    ang_hi = (jnp.arange(seq // tm, dtype=F32) * tm)[:, None] * inv_freq[None, :]
    tables = (jnp.cos(ang_lo), jnp.sin(ang_lo), jnp.cos(ang_hi), jnp.sin(ang_hi))

    proj, small = _inproj(x2, n_mix_pre.reshape(1, d), w16, w_right, w_small, tables, tm, seq)

    small_t = jnp.transpose(small[:, :2 * A_HEADS].reshape(batch, seq, 2 * A_HEADS), (0, 2, 1))
    alog_b = jnp.broadcast_to(a_log.reshape(A_HEADS, 1), (A_HEADS, SB))
    dtb_b = jnp.broadcast_to(dt_bias.reshape(A_HEADS, 1), (A_HEADS, SB))
    oa = _gdn(proj, small_t, alog_b, dtb_b, conv_a, norm_a.reshape(1, A_DV), batch, seq)

    lb = _pick(seq, (2048, 1024, 512, 256))
    ob = _ret(proj, norm_b.reshape(1, B_V), batch, seq, lb)

    tm2 = _pick(t, (512, 256))
    x1 = _merge(oa, ob, proj, x2, w_br_a.astype(BF16), w_br_b.astype(BF16), w_out.astype(BF16),
                n_mix_post.reshape(1, d), tm2)
    out = _mlp(x1, n_mlp_pre.reshape(1, d), w_up.astype(BF16), w_down.astype(BF16),
               n_mlp_post.reshape(1, d), tm2, 1024)
    return out.reshape(batch, seq, d)


def kernel(x, norm_mix_pre, norm_mix_post, norm_mlp_pre, norm_mlp_post, w_in, conv_a, a_log,
           dt_bias, norm_a, norm_b, w_br_a, w_br_b, w_out, w_up, w_down):
    for l in range(w_in.shape[0]):
        x = _layer(x, norm_mix_pre[l], norm_mix_post[l], norm_mlp_pre[l], norm_mlp_post[l],
                   w_in[l], conv_a[l], a_log[l], dt_bias[l], norm_a[l], norm_b[l],
                   w_br_a[l], w_br_b[l], w_out[l], w_up[l], w_down[l])
    return x
```

```python
import functools

import jax
import jax.numpy as jnp
from jax import lax
from jax.experimental import pallas as pl
from jax.experimental.pallas import tpu as pltpu

F32 = jnp.float32
BF16 = jnp.bfloat16

CHUNK = 64
EPS = 1e-6
A_HEADS, A_DK, A_DV, CONV_W = 8, 128, 256, 4
B_HEADS, B_DK, B_DV = 4, 256, 512
ROPE_BASE = 10000.0
A_QK, A_V = A_HEADS * A_DK, A_HEADS * A_DV
B_QK, B_V = B_HEADS * B_DK, B_HEADS * B_DV

SB = 256
CPS = SB // CHUNK
HB = SB // 2
LANES = 128
HG = 8
VMEM_LIMIT = 56 * 1024 * 1024

_PW = 2048
_T_SPLIT = (2 * A_QK + 2 * A_V) // _PW
_T_ROT = _T_SPLIT
_T_GB = (2 * A_QK + 2 * A_V + 2 * B_QK + B_V) // _PW

_NT = (((1,), (1,)), ((), ()))
_TN = (((0,), (0,)), ((), ()))


def _dot(a, b):
    return jnp.dot(a, b, preferred_element_type=F32)


def _silu(x):
    h = 0.5 * x
    return h * (1.0 + jnp.tanh(h))


def _inproj_kernel(x_ref, g_ref, wl_ref, wr_ref, ws_ref, cl_ref, sl_ref, ch_ref, sh_ref,
                   o_ref, os_ref, xn_ref, *, tiles_per_seq):
    i = pl.program_id(0)
    j = pl.program_id(1)

    @pl.when(j == 0)
    def _():
        x = x_ref[...]
        ms = jnp.mean(x * x, axis=-1, keepdims=True)
        xn = (x * lax.rsqrt(ms + EPS) * g_ref[...]).astype(BF16)
        xn_ref[...] = xn
        os_ref[...] = _dot(xn, ws_ref[...])

    @pl.when(j < _T_SPLIT)
    def _():
        o_ref[...] = _dot(xn_ref[...], wl_ref[...]).astype(o_ref.dtype)

    @pl.when(j == _T_ROT)
    def _():
        acc = _dot(xn_ref[...], wr_ref[...])
        k = i % tiles_per_seq
        ch = ch_ref[pl.ds(k, 1), :]
        sh = sh_ref[pl.ds(k, 1), :]
        cos = ch * cl_ref[...] - sh * sl_ref[...]
        sin = sh * cl_ref[...] + ch * sl_ref[...]
        half = B_DK // 2
        for hd in range(_PW // B_DK):
            x1 = acc[:, hd * B_DK:hd * B_DK + half]
            x2 = acc[:, hd * B_DK + half:(hd + 1) * B_DK]
            o_ref[:, hd * B_DK:hd * B_DK + half] = (x1 * cos - x2 * sin).astype(o_ref.dtype)
            o_ref[:, hd * B_DK + half:(hd + 1) * B_DK] = (x1 * sin + x2 * cos).astype(o_ref.dtype)

    @pl.when(j == _T_GB)
    def _():
        o_ref[...] = _silu(_dot(xn_ref[...], wr_ref[...])).astype(o_ref.dtype)

    @pl.when((j >= _T_SPLIT) & (j != _T_ROT) & (j != _T_GB))
    def _():
        o_ref[...] = _dot(xn_ref[...], wr_ref[...]).astype(o_ref.dtype)


def _inproj(x2, gain, w_left, w_right, w_small, tables, tm, seq):
    t, d = x2.shape
    n_right = w_right.shape[1] // _PW
    n = (_T_SPLIT + n_right) * _PW
    assert w_left.shape[1] >= _T_SPLIT * _PW and w_right.shape[1] % _PW == 0 and seq % tm == 0
    tiles_per_seq = seq // tm
    cos_lo, sin_lo, cos_hi, sin_hi = tables
    const = lambda i, j: (0, 0)
    return pl.pallas_call(
        functools.partial(_inproj_kernel, tiles_per_seq=tiles_per_seq),
        grid=(t // tm, n // _PW),
        in_specs=[
            pl.BlockSpec((tm, d), lambda i, j: (i, 0)),
            pl.BlockSpec((1, d), const),
            pl.BlockSpec((d, _PW), lambda i, j: (0, jnp.minimum(j, _T_SPLIT - 1))),
            pl.BlockSpec((d, _PW), lambda i, j: (0, jnp.where(j < _T_SPLIT, n_right - 1, j - _T_SPLIT))),
            pl.BlockSpec((d, LANES), const),
            pl.BlockSpec((tm, B_DK // 2), const),
            pl.BlockSpec((tm, B_DK // 2), const),
            pl.BlockSpec((tiles_per_seq, B_DK // 2), const),
            pl.BlockSpec((tiles_per_seq, B_DK // 2), const),
        ],
        out_specs=[
            pl.BlockSpec((tm, _PW), lambda i, j: (i, j)),
            pl.BlockSpec((tm, LANES), lambda i, j: (i, 0)),
        ],
        out_shape=[
            jax.ShapeDtypeStruct((t, n), BF16),
            jax.ShapeDtypeStruct((t, LANES), F32),
        ],
        scratch_shapes=[pltpu.VMEM((tm, d), BF16)],
        compiler_params=pltpu.CompilerParams(
            dimension_semantics=("arbitrary", "arbitrary"), vmem_limit_bytes=VMEM_LIMIT),
        name="inproj",
    )(x2, gain, w_left, w_right, w_small, cos_lo, sin_lo, cos_hi, sin_hi)


def _conv_silu(tail_ref, x_ref, c_ref):
    x = x_ref[...].astype(F32)
    rows = x.shape[0]
    xe = jnp.concatenate([tail_ref[...], x], axis=0)
    tail_ref[...] = x[rows - 8:, :]
    c_half = 0.5 * c_ref[...]
    h = x * c_half[CONV_W - 1:CONV_W, :]
    for d in range(1, CONV_W):
        h = h + pltpu.roll(xe, d, axis=0)[8:, :] * c_half[CONV_W - 1 - d:CONV_W - d, :]
    return h * (1.0 + jnp.tanh(h))


def _l2norm(y):
    return y * lax.rsqrt(jnp.sum(y * y, axis=-1, keepdims=True) + EPS)


def _group_cumsum_rows(x, group):
    lane = lax.broadcasted_iota(jnp.int32, x.shape, 1)
    pos = lane & (group - 1)
    s = 1
    while s < group:
        shifted = pltpu.roll(x, s, axis=1)
        x = x + jnp.where(pos >= s, shifted, 0.0)
        s *= 2
    return x


_B_NEG16, _B_C1, _B_C2, _B_EYE, _B_CAUSAL, _N_BMASKS = 0, 1, 2, 3, 4, 5


def _col_replicated(row):
    return jnp.transpose(jnp.broadcast_to(row, (LANES, SB)))


def _gdn_prep(grp, q_ref, k_ref, v_ref, st_ref, alog_ref, dtb_ref, cq_ref, ck_ref, cv_ref,
              eq_ref, ek_ref, ev_ref, dst):
    lq_ref, kk_ref, rhs_ref, qe_ref, kdt_ref, gcc_ref, grs_ref = dst
    qa = _conv_silu(eq_ref, q_ref, cq_ref)
    ka = _conv_silu(ek_ref, k_ref, ck_ref)
    va = _conv_silu(ev_ref, v_ref, cv_ref)
    for j in range(HG):
        hh = grp * HG + j
        beta_row = jax.nn.sigmoid(st_ref[pl.ds(hh, 1), :])
        xg = st_ref[pl.ds(A_HEADS + hh, 1), :] + dtb_ref[pl.ds(hh, 1), :]
        softplus = jnp.maximum(xg, 0.0) + jnp.log1p(jnp.exp(-jnp.abs(xg)))
        g_row = -jnp.exp(alog_ref[pl.ds(hh, 1), :]) * softplus
        gc_row8 = _group_cumsum_rows(jnp.broadcast_to(g_row, (8, SB)), CHUNK)
        gc = _col_replicated(gc_row8[0:1])
        bc = _col_replicated(beta_row)
        grs_ref[j] = gc_row8
        gcc_ref[j] = gc

        q = _l2norm(qa[:, j * A_DK:(j + 1) * A_DK]) * (A_DK ** -0.5)
        k = _l2norm(ka[:, j * A_DK:(j + 1) * A_DK])
        v = va[:, j * A_DV:(j + 1) * A_DV]
        eg = jnp.exp(gc)
        kb = k * bc
        glast_rep = jnp.concatenate(
            [jnp.broadcast_to(gc[c * CHUNK + CHUNK - 1:(c + 1) * CHUNK, :], (CHUNK, LANES))
             for c in range(CPS)], axis=0)
        for b in range(2):
            lq_ref[j, b] = jnp.concatenate([kb[b * HB:(b + 1) * HB], q[b * HB:(b + 1) * HB]],
                                           axis=0).astype(BF16)
        kk_ref[j] = k.astype(BF16)
        rhs_ref[j] = jnp.concatenate([v * jnp.concatenate([bc, bc], axis=1), kb * eg],
                                     axis=1).astype(BF16)
        qe_ref[j] = (q * eg).astype(BF16)
        kdt_ref[j] = jnp.transpose(k * jnp.exp(glast_rep - gc)).astype(BF16)


def _gdn_main(fresh, src, z_ref, na_ref, o_ref, state_ref, bmask_ref, sel_ref):
    lq_ref, kk_ref, rhs_ref, qe_ref, kdt_ref, gcc_ref, grs_ref = src
    heads = range(HG)

    kq = [[lax.dot_general(lq_ref[j, b], kk_ref[j, b * HB:(b + 1) * HB, :], _NT,
                           preferred_element_type=F32) for b in range(2)] for j in heads]

    def bdot(a, b):
        return [_dot(a[h], b[h]).astype(BF16) for h in range(2)]

    def masked_scores(j):
        gc, gr = gcc_ref[j], grs_ref[j][0:1]
        out = []
        for b in range(2):
            rs = slice(b * HB, (b + 1) * HB)
            decay = jnp.exp(jnp.minimum(gc[rs] - gr[:, rs], 0.0))
            ab = (kq[j][b][:HB] * decay).astype(BF16)
            n1 = ab * bmask_ref[_B_NEG16]
            qk = kq[j][b][HB:]
            out.append(((qk * decay).astype(BF16) * bmask_ref[_B_CAUSAL], n1,
                        bmask_ref[_B_EYE] + n1, ab * bmask_ref[_B_C1], ab * bmask_ref[_B_C2]))
        return [list(t) for t in zip(*out)]

    attn, n1b, pb, c1b, c2b = zip(*[masked_scores(j) for j in heads])
    eye = bmask_ref[_B_EYE]

    n2b = [bdot(n1b[j], n1b[j]) for j in heads]
    pb = [bdot(pb[j], [eye + m for m in n2b[j]]) for j in heads]
    n4b = [bdot(n2b[j], n2b[j]) for j in heads]
    pb = [bdot(pb[j], [eye + m for m in n4b[j]]) for j in heads]
    n8b = [bdot(n4b[j], n4b[j]) for j in heads]
    pb = [bdot(pb[j], [eye + m for m in n8b[j]]) for j in heads]
    tinv = pb
    for blk, cb in ((16, c1b), (32, c2b)):
        low = lambda x: jnp.concatenate(
            [x[b + blk:b + 2 * blk] for b in range(0, HB, 2 * blk)], axis=0)
        eye_low = low(eye)
        xm = [[eye_low - m for m in bdot([low(t) for t in tinv[j]], cb[j])] for j in heads]
        t_low = [bdot(xm[j], tinv[j]) for j in heads]
        tinv = [[jnp.concatenate(
            [piece for i, b in enumerate(range(0, HB, 2 * blk))
             for piece in (tinv[j][h][b:b + blk], t_low[j][h][i * blk:(i + 1) * blk])], axis=0)
            for h in range(2)] for j in heads]
    halves = range(2)
    uwb = [[_dot(tinv[j][h], rhs_ref[j, h * HB:(h + 1) * HB, :]).astype(BF16) for h in halves]
           for j in heads]
    aw = [[_dot(attn[j][h], uwb[j][h]) for h in halves] for j in heads]
    qeff = [jnp.concatenate(
        [(qe_ref[j, h * HB:(h + 1) * HB, :].astype(F32) - aw[j][h][:, A_DV:]).astype(BF16)
         for h in halves], axis=0) for j in heads]
    bp = [[_dot(jnp.concatenate([kdt_ref[j, :, h * HB:(h + 1) * HB]] * (CPS // 2), axis=0)
                * sel_ref[...], uwb[j][h]) for h in halves] for j in heads]

    st = [jnp.where(fresh, 0.0, state_ref[j]) for j in heads]
    outs = [[] for _ in heads]
    for c in range(CPS):
        for j in heads:
            blk = bp[j][c // 2][(c % 2) * A_DK:(c % 2 + 1) * A_DK]
            lhs = jnp.concatenate([blk[:, A_DV:].astype(BF16), qeff[j][c * CHUNK:(c + 1) * CHUNK]],
                                  axis=0)
            res = _dot(lhs, st[j].astype(BF16))
            outs[j].append(res[A_DK:])
            ld = jnp.exp(gcc_ref[j, c * CHUNK + CHUNK - 1:(c + 1) * CHUNK, :])
            st[j] = st[j] * jnp.concatenate([ld, ld], axis=1) + blk[:, :A_DV] - res[:A_DK]

    na = na_ref[...]
    for j in heads:
        state_ref[j] = st[j]
        o = jnp.concatenate(outs[j], axis=0) + jnp.concatenate([a[:, :A_DV] for a in aw[j]], axis=0)
        ms = jnp.mean(o * o, axis=-1, keepdims=True)
        z = z_ref[:, j * A_DV:(j + 1) * A_DV].astype(F32)
        o_ref[:, j * A_DV:(j + 1) * A_DV] = (
            o * lax.rsqrt(ms + EPS) * na * _silu(z)).astype(o_ref.dtype)


def _gdn_kernel(q_ref, k_ref, v_ref, z_ref, st_ref, alog_ref, dtb_ref, cq_ref, ck_ref, cv_ref,
                na_ref, o_ref, eq_ref, ek_ref, ev_ref, state_ref, bmask_ref, sel_ref, *stage):
    grp = pl.program_id(1)
    r = pl.program_id(2)
    stage_a, stage_b = stage[:len(stage) // 2], stage[len(stage) // 2:]

    @pl.when((pl.program_id(0) == 0) & (grp == 0) & (r == 0))
    def _():
        row = lax.broadcasted_iota(jnp.int32, (HB, HB), 0)
        col = lax.broadcasted_iota(jnp.int32, (HB, HB), 1)

        def same(bits):
            return ((row >> bits) == (col >> bits)).astype(F32)

        m16, m32, m64 = same(4), same(5), same(6)
        lower = (row >= col).astype(F32)
        bmask_ref[_B_CAUSAL] = (m64 * lower).astype(BF16)
        bmask_ref[_B_NEG16] = (-(m16 * (row > col).astype(F32))).astype(BF16)
        bmask_ref[_B_C1] = ((m32 - m16) * lower).astype(BF16)
        bmask_ref[_B_C2] = ((m64 - m32) * lower).astype(BF16)
        bmask_ref[_B_EYE] = (row == col).astype(F32).astype(BF16)
        rsel = lax.broadcasted_iota(jnp.int32, (CPS // 2 * A_DK, HB), 0)
        csel = lax.broadcasted_iota(jnp.int32, (CPS // 2 * A_DK, HB), 1)
        sel_ref[...] = ((rsel >> 7) == (csel >> 6)).astype(F32).astype(BF16)
        for ref in stage_b:
            ref[...] = jnp.zeros_like(ref)
        state_ref[...] = jnp.zeros_like(state_ref)

    @pl.when(r == 0)
    def _():
        eq_ref[...] = jnp.zeros_like(eq_ref)
        ek_ref[...] = jnp.zeros_like(ek_ref)
        ev_ref[...] = jnp.zeros_like(ev_ref)

    def step(dst, src):
        _gdn_prep(grp, q_ref, k_ref, v_ref, st_ref, alog_ref, dtb_ref, cq_ref, ck_ref, cv_ref,
                  eq_ref, ek_ref, ev_ref, dst)
        _gdn_main(r <= 1, src, z_ref, na_ref, o_ref, state_ref, bmask_ref, sel_ref)

    @pl.when(r % 2 == 0)
    def _():
        step(stage_a, stage_b)

    @pl.when(r % 2 == 1)
    def _():
        step(stage_b, stage_a)


def _gdn(proj, small_t, alog_b, dtb_b, conv_a, norm_a, batch, seq):
    t = proj.shape[0]
    nr = seq // SB
    wq, wv = HG * A_DK, HG * A_DV
    k_off = A_QK // wq
    v_off = (2 * A_QK) // wv
    z_off = (2 * A_QK + A_V) // wv

    def staged(b, g, r):
        return b * nr + jnp.minimum(r, nr - 1)

    def solved(b, g, r):
        return b * nr + jnp.maximum(r - 1, 0)

    stage = [
        pltpu.VMEM((HG, 2, 2 * HB, A_DK), BF16),
        pltpu.VMEM((HG, SB, A_DK), BF16),
        pltpu.VMEM((HG, SB, A_DV + A_DK), BF16),
        pltpu.VMEM((HG, SB, A_DK), BF16),
        pltpu.VMEM((HG, A_DK, SB), BF16),
        pltpu.VMEM((HG, SB, LANES), F32),
        pltpu.VMEM((HG, 8, SB), F32),
    ]
    return pl.pallas_call(
        _gdn_kernel,
        grid=(batch, A_HEADS // HG, nr + 1),
        in_specs=[
            pl.BlockSpec((SB, wq), lambda b, g, r: (staged(b, g, r), g)),
            pl.BlockSpec((SB, wq), lambda b, g, r: (staged(b, g, r), k_off + g)),
            pl.BlockSpec((SB, wv), lambda b, g, r: (staged(b, g, r), v_off + g)),
            pl.BlockSpec((SB, wv), lambda b, g, r: (solved(b, g, r), z_off + g)),
            pl.BlockSpec((None, 2 * A_HEADS, SB), lambda b, g, r: (b, 0, jnp.minimum(r, nr - 1))),
            pl.BlockSpec((A_HEADS, SB), lambda b, g, r: (0, 0)),
            pl.BlockSpec((A_HEADS, SB), lambda b, g, r: (0, 0)),
            pl.BlockSpec((CONV_W, wq), lambda b, g, r: (0, g)),
            pl.BlockSpec((CONV_W, wq), lambda b, g, r: (0, k_off + g)),
            pl.BlockSpec((CONV_W, wv), lambda b, g, r: (0, v_off + g)),
            pl.BlockSpec((1, A_DV), lambda b, g, r: (0, 0)),
        ],
        out_specs=pl.BlockSpec((SB, wv), lambda b, g, r: (solved(b, g, r), g)),
        out_shape=jax.ShapeDtypeStruct((t, A_V), BF16),
        scratch_shapes=[
            pltpu.VMEM((8, wq), F32),
            pltpu.VMEM((8, wq), F32),
            pltpu.VMEM((8, wv), F32),
            pltpu.VMEM((HG, A_DK, A_DV), F32),
            pltpu.VMEM((_N_BMASKS, HB, HB), BF16),
            pltpu.VMEM((CPS // 2 * A_DK, HB), BF16),
        ] + stage + stage,
        compiler_params=pltpu.CompilerParams(
            dimension_semantics=("arbitrary", "arbitrary", "arbitrary"),
            vmem_limit_bytes=VMEM_LIMIT),
        name="gdn",
    )(proj, proj, proj, proj, small_t, alog_b, dtb_b, conv_a, conv_a, conv_a, norm_a)


_D_MASK, _D_Q, _D_K, _D_SB, _N_DEC = 0, 1, 2, 3, 4


def _ret_kernel(q_ref, k_ref, v_ref, g_ref, nb_ref, o_ref, state_ref, dec_ref):
    h = pl.program_id(1)
    r = pl.program_id(2)
    lb = q_ref.shape[0]
    nsb = lb // SB

    @pl.when(r == 0)
    def _():
        state_ref[...] = jnp.zeros_like(state_ref)
        hf = jnp.full((SB, SB), h, jnp.int32).astype(F32)
        lg = jnp.log1p(-jnp.exp2(-5.0 - hf))
        row = lax.broadcasted_iota(jnp.int32, (SB, SB), 0)
        col = lax.broadcasted_iota(jnp.int32, (SB, SB), 1)
        allow = ((col >> 6) <= (row >> 6)).astype(F32)
        dist = jnp.abs(row - col).astype(F32)
        rowf = row.astype(F32)
        dec_ref[_D_MASK] = jnp.exp(lg * dist) * allow * (B_DK ** -0.5)
        dec_ref[_D_Q] = jnp.exp(lg * (rowf + 1.0))
        dec_ref[_D_K] = jnp.exp(lg * (SB - 1.0 - rowf)) * (B_DK ** -0.5)
        dec_ref[_D_SB] = jnp.exp(lg * SB)

    mask = dec_ref[_D_MASK]
    q_dec = dec_ref[_D_Q]
    k_dec = dec_ref[_D_K]
    sb_dec = dec_ref[_D_SB][0:1, 0:1]
    nb = nb_ref[...]

    state = state_ref[...]
    for s in range(nsb):
        r0 = s * SB
        qb = q_ref[pl.ds(r0, SB), :]
        kb = k_ref[pl.ds(r0, SB), :]
        vb = v_ref[pl.ds(r0, SB), :]
        sc = lax.dot_general(qb, kb, _NT, preferred_element_type=F32)
        o = (_dot((sc * mask).astype(BF16), vb)
             + _dot((qb.astype(F32) * q_dec).astype(BF16), state.astype(BF16)))
        state = state * sb_dec + lax.dot_general(
            (kb.astype(F32) * k_dec).astype(BF16), vb, _TN, preferred_element_type=F32)
        ms = jnp.mean(o * o, axis=-1, keepdims=True)
        gate = g_ref[pl.ds(r0, SB), :].astype(F32)
        o_ref[pl.ds(r0, SB), :] = (o * lax.rsqrt(ms + EPS) * nb * gate).astype(o_ref.dtype)
    state_ref[...] = state


def _ret(proj, norm_b, batch, seq, lb):
    t = proj.shape[0]
    nr = seq // lb
    base = 2 * A_QK + 2 * A_V
    q_off = base // B_DK
    k_off = (base + B_QK) // B_DK
    v_off = (base + 2 * B_QK) // B_DV
    g_off = (base + 2 * B_QK + B_V) // B_DV

    def rows(b, h, r):
        return b * nr + r

    return pl.pallas_call(
        _ret_kernel,
        grid=(batch, B_HEADS, nr),
        in_specs=[
            pl.BlockSpec((lb, B_DK), lambda b, h, r: (rows(b, h, r), q_off + h)),
            pl.BlockSpec((lb, B_DK), lambda b, h, r: (rows(b, h, r), k_off + h)),
            pl.BlockSpec((lb, B_DV), lambda b, h, r: (rows(b, h, r), v_off + h)),
            pl.BlockSpec((lb, B_DV), lambda b, h, r: (rows(b, h, r), g_off + h)),
            pl.BlockSpec((1, B_DV), lambda b, h, r: (0, h)),
        ],
        out_specs=pl.BlockSpec((lb, B_DV), lambda b, h, r: (rows(b, h, r), h)),
        out_shape=jax.ShapeDtypeStruct((t, B_V), BF16),
        scratch_shapes=[pltpu.VMEM((B_DK, B_DV), F32), pltpu.VMEM((_N_DEC, SB, SB), F32)],
        compiler_params=pltpu.CompilerParams(
            dimension_semantics=("arbitrary", "arbitrary", "arbitrary"),
            vmem_limit_bytes=VMEM_LIMIT),
        name="ret",
    )(proj, proj, proj, proj, norm_b)


def _merge_kernel(oa_ref, ob_ref, ga_ref, gb_ref, x_ref, wa_ref, wb_ref, wo_ref, n_ref, o_ref):
    ya = _dot(oa_ref[...], wa_ref[...])
    yb = _dot(ob_ref[...], wb_ref[...])
    ga = jax.nn.sigmoid(ga_ref[...].astype(F32))
    gb = jax.nn.sigmoid(gb_ref[...].astype(F32))
    mix = _dot((ga * ya + gb * yb).astype(BF16), wo_ref[...])
    ms = jnp.mean(mix * mix, axis=-1, keepdims=True)
    o_ref[...] = x_ref[...] + mix * lax.rsqrt(ms + EPS) * n_ref[...]


def _merge(oa, ob, proj, x2, wa, wb, wo, gain, tm):
    t, d = x2.shape
    gate_off = (2 * A_QK + 2 * A_V + 2 * B_QK + 2 * B_V) // d
    const = lambda i: (0, 0)
    return pl.pallas_call(
        _merge_kernel,
        grid=(t // tm,),
        in_specs=[
            pl.BlockSpec((tm, A_V), lambda i: (i, 0)),
            pl.BlockSpec((tm, B_V), lambda i: (i, 0)),
            pl.BlockSpec((tm, d), lambda i: (i, gate_off)),
            pl.BlockSpec((tm, d), lambda i: (i, gate_off + 1)),
            pl.BlockSpec((tm, d), lambda i: (i, 0)),
            pl.BlockSpec((A_V, d), const),
            pl.BlockSpec((B_V, d), const),
            pl.BlockSpec((d, d), const),
            pl.BlockSpec((1, d), const),
        ],
        out_specs=pl.BlockSpec((tm, d), lambda i: (i, 0)),
        out_shape=jax.ShapeDtypeStruct((t, d), F32),
        compiler_params=pltpu.CompilerParams(
            dimension_semantics=("arbitrary",), vmem_limit_bytes=VMEM_LIMIT),
        name="merge",
    )(oa, ob, proj, proj, x2, wa, wb, wo, gain)


def _mlp_kernel(x_ref, gpre_ref, wu_ref, wd_ref, gpost_ref, o_ref, *, ff_tile):
    x = x_ref[...]
    ms = jnp.mean(x * x, axis=-1, keepdims=True)
    hn = (x * lax.rsqrt(ms + EPS) * gpre_ref[...]).astype(BF16)
    d_ff = wu_ref.shape[1]
    ff = jnp.zeros(x.shape, F32)
    for j in range(d_ff // ff_tile):
        up = _dot(hn, wu_ref[:, j * ff_tile:(j + 1) * ff_tile])
        act = jnp.square(jnp.maximum(up, 0.0)).astype(BF16)
        ff = ff + _dot(act, wd_ref[j * ff_tile:(j + 1) * ff_tile, :])
    ms2 = jnp.mean(ff * ff, axis=-1, keepdims=True)
    o_ref[...] = x + ff * lax.rsqrt(ms2 + EPS) * gpost_ref[...]


def _mlp(x1, gpre, wu, wd, gpost, tm, ff_tile):
    t, d = x1.shape
    d_ff = wu.shape[1]
    const = lambda i: (0, 0)
    return pl.pallas_call(
        functools.partial(_mlp_kernel, ff_tile=ff_tile),
        grid=(t // tm,),
        in_specs=[
            pl.BlockSpec((tm, d), lambda i: (i, 0)),
            pl.BlockSpec((1, d), const),
            pl.BlockSpec((d, d_ff), const),
            pl.BlockSpec((d_ff, d), const),
            pl.BlockSpec((1, d), const),
        ],
        out_specs=pl.BlockSpec((tm, d), lambda i: (i, 0)),
        out_shape=jax.ShapeDtypeStruct((t, d), F32),
        compiler_params=pltpu.CompilerParams(
            dimension_semantics=("arbitrary",), vmem_limit_bytes=VMEM_LIMIT),
        name="mlp",
    )(x1, gpre, wu, wd, gpost)


def _pick(n, candidates):
    for c in candidates:
        if n % c == 0:
            return c
    raise ValueError(f"no tile for {n} among {candidates}")


def _layer(x, n_mix_pre, n_mix_post, n_mlp_pre, n_mlp_post, w_in, conv_a, a_log, dt_bias,
           norm_a, norm_b, w_br_a, w_br_b, w_out, w_up, w_down):
    batch, seq, d = x.shape
    t = batch * seq
    assert seq % SB == 0
    x2 = x.reshape(t, d)

    n_small_lo = 2 * A_QK + 2 * A_V
    n_small_hi = n_small_lo + 2 * A_HEADS
    w16 = w_in.astype(BF16)
    w_right = w16[:, n_small_hi:]
    w_small = jnp.pad(w16[:, n_small_lo:n_small_hi], ((0, 0), (0, LANES - 2 * A_HEADS)))

    tm = _pick(seq, (1024, 512, 256))
    inv_freq = ROPE_BASE ** (-jnp.arange(0, B_DK, 2, dtype=F32) / B_DK)
    ang_lo = jnp.arange(tm, dtype=F32)[:, None] * inv_freq[None, :]
    ang_hi = (jnp.arange(seq // tm, dtype=F32) * tm)[:, None] * inv_freq[None, :]
    tables = (jnp.cos(ang_lo), jnp.sin(ang_lo), jnp.cos(ang_hi), jnp.sin(ang_hi))

    proj, small = _inproj(x2, n_mix_pre.reshape(1, d), w16, w_right, w_small, tables, tm, seq)

    small_t = jnp.transpose(small[:, :2 * A_HEADS].reshape(batch, seq, 2 * A_HEADS), (0, 2, 1))
    alog_b = jnp.broadcast_to(a_log.reshape(A_HEADS, 1), (A_HEADS, SB))
    dtb_b = jnp.broadcast_to(dt_bias.reshape(A_HEADS, 1), (A_HEADS, SB))
    oa = _gdn(proj, small_t, alog_b, dtb_b, conv_a, norm_a.reshape(1, A_DV), batch, seq)

    lb = _pick(seq, (2048, 1024, 512, 256))
    ob = _ret(proj, norm_b.reshape(1, B_V), batch, seq, lb)

    tm2 = _pick(t, (512, 256))
    x1 = _merge(oa, ob, proj, x2, w_br_a.astype(BF16), w_br_b.astype(BF16), w_out.astype(BF16),
                n_mix_post.reshape(1, d), tm2)
    out = _mlp(x1, n_mlp_pre.reshape(1, d), w_up.astype(BF16), w_down.astype(BF16),
               n_mlp_post.reshape(1, d), tm2, 1024)
    return out.reshape(batch, seq, d)


def kernel(x, norm_mix_pre, norm_mix_post, norm_mlp_pre, norm_mlp_post, w_in, conv_a, a_log,
           dt_bias, norm_a, norm_b, w_br_a, w_br_b, w_out, w_up, w_down):
    for l in range(w_in.shape[0]):
        x = _layer(x, norm_mix_pre[l], norm_mix_post[l], norm_mlp_pre[l], norm_mlp_post[l],
                   w_in[l], conv_a[l], a_log[l], dt_bias[l], norm_a[l], norm_b[l],
                   w_br_a[l], w_br_b[l], w_out[l], w_up[l], w_down[l])
    return x
```

```python
import functools
import math

import jax
import jax.numpy as jnp
from jax import lax
from jax.experimental import pallas as pl
from jax.experimental.pallas import tpu as pltpu

F32 = jnp.float32
BF16 = jnp.bfloat16

CHUNK = 64
EPS = 1e-6
A_HEADS, A_DK, A_DV, CONV_W = 8, 128, 256, 4
B_HEADS, B_DK, B_DV = 4, 256, 512
ROPE_BASE = 10000.0
A_QK, A_V = A_HEADS * A_DK, A_HEADS * A_DV
B_QK, B_V = B_HEADS * B_DK, B_HEADS * B_DV

SB = 256
CPS = SB // CHUNK
HB = SB // 2
LANES = 128
HG = 8
VMEM_LIMIT = 56 * 1024 * 1024

_PW = 2048
_T_SPLIT = (2 * A_QK + 2 * A_V) // _PW
_T_ROT = _T_SPLIT
_T_Z = (2 * A_QK + A_V) // _PW
_T_GB = (2 * A_QK + 2 * A_V + 2 * B_QK + B_V) // _PW

_NT = (((1,), (1,)), ((), ()))
_TN = (((0,), (0,)), ((), ()))


def _dot(a, b):
    return jnp.dot(a, b, preferred_element_type=F32)


def _silu(x):
    h = 0.5 * x
    return h * (1.0 + jnp.tanh(h))


def _inproj_kernel(x_ref, g_ref, wl_ref, wr_ref, ws_ref, cl_ref, sl_ref, ch_ref, sh_ref,
                   o_ref, os_ref, xn_ref, *, tiles_per_seq):
    i = pl.program_id(0)
    j = pl.program_id(1)

    @pl.when(j == 0)
    def _():
        x = x_ref[...]
        ms = jnp.mean(x * x, axis=-1, keepdims=True)
        xn = (x * lax.rsqrt(ms + EPS) * g_ref[...]).astype(BF16)
        xn_ref[...] = xn
        os_ref[...] = _dot(xn, ws_ref[...])

    @pl.when(j < _T_Z)
    def _():
        o_ref[...] = _dot(xn_ref[...], wl_ref[...]).astype(o_ref.dtype)

    @pl.when(j == _T_Z)
    def _():
        o_ref[...] = _silu(_dot(xn_ref[...], wl_ref[...])).astype(o_ref.dtype)

    @pl.when(j == _T_ROT)
    def _():
        acc = _dot(xn_ref[...], wr_ref[...])
        k = i % tiles_per_seq
        ch = ch_ref[pl.ds(k, 1), :]
        sh = sh_ref[pl.ds(k, 1), :]
        cos = ch * cl_ref[...] - sh * sl_ref[...]
        sin = sh * cl_ref[...] + ch * sl_ref[...]
        half = B_DK // 2
        for hd in range(_PW // B_DK):
            x1 = acc[:, hd * B_DK:hd * B_DK + half]
            x2 = acc[:, hd * B_DK + half:(hd + 1) * B_DK]
            o_ref[:, hd * B_DK:hd * B_DK + half] = (x1 * cos - x2 * sin).astype(o_ref.dtype)
            o_ref[:, hd * B_DK + half:(hd + 1) * B_DK] = (x1 * sin + x2 * cos).astype(o_ref.dtype)

    @pl.when(j == _T_GB)
    def _():
        o_ref[...] = _silu(_dot(xn_ref[...], wr_ref[...])).astype(o_ref.dtype)

    @pl.when((j >= _T_SPLIT) & (j != _T_ROT) & (j != _T_GB))
    def _():
        o_ref[...] = _dot(xn_ref[...], wr_ref[...]).astype(o_ref.dtype)


def _inproj(x2, gain, w_left, w_right, w_small, tables, tm, seq):
    t, d = x2.shape
    n_right = w_right.shape[1] // _PW
    n = (_T_SPLIT + n_right) * _PW
    assert w_left.shape[1] >= _T_SPLIT * _PW and w_right.shape[1] % _PW == 0 and seq % tm == 0
    tiles_per_seq = seq // tm
    cos_lo, sin_lo, cos_hi, sin_hi = tables
    const = lambda i, j: (0, 0)
    return pl.pallas_call(
        functools.partial(_inproj_kernel, tiles_per_seq=tiles_per_seq),
        grid=(t // tm, n // _PW),
        in_specs=[
            pl.BlockSpec((tm, d), lambda i, j: (i, 0)),
            pl.BlockSpec((1, d), const),
            pl.BlockSpec((d, _PW), lambda i, j: (0, jnp.minimum(j, _T_SPLIT - 1))),
            pl.BlockSpec((d, _PW), lambda i, j: (0, jnp.where(j < _T_SPLIT, n_right - 1, j - _T_SPLIT))),
            pl.BlockSpec((d, LANES), const),
            pl.BlockSpec((tm, B_DK // 2), const),
            pl.BlockSpec((tm, B_DK // 2), const),
            pl.BlockSpec((tiles_per_seq, B_DK // 2), const),
            pl.BlockSpec((tiles_per_seq, B_DK // 2), const),
        ],
        out_specs=[
            pl.BlockSpec((tm, _PW), lambda i, j: (i, j)),
            pl.BlockSpec((tm, LANES), lambda i, j: (i, 0)),
        ],
        out_shape=[
            jax.ShapeDtypeStruct((t, n), BF16),
            jax.ShapeDtypeStruct((t, LANES), F32),
        ],
        scratch_shapes=[pltpu.VMEM((tm, d), BF16)],
        compiler_params=pltpu.CompilerParams(
            dimension_semantics=("arbitrary", "arbitrary"), vmem_limit_bytes=VMEM_LIMIT),
        name="inproj",
    )(x2, gain, w_left, w_right, w_small, cos_lo, sin_lo, cos_hi, sin_hi)


def _conv_silu(tail_ref, x_ref, c_ref):
    x = x_ref[...].astype(F32)
    rows = x.shape[0]
    xe = jnp.concatenate([tail_ref[...], x], axis=0)
    tail_ref[...] = x[rows - 8:, :]
    c_half = 0.5 * c_ref[...]
    h = x * c_half[CONV_W - 1:CONV_W, :]
    for d in range(1, CONV_W):
        h = h + pltpu.roll(xe, d, axis=0)[8:, :] * c_half[CONV_W - 1 - d:CONV_W - d, :]
    return h * (1.0 + jnp.tanh(h))


def _l2norm(y):
    return y * lax.rsqrt(jnp.sum(y * y, axis=-1, keepdims=True) + EPS)


def _group_cumsum_rows(x, group):
    lane = lax.broadcasted_iota(jnp.int32, x.shape, 1)
    pos = lane & (group - 1)
    s = 1
    while s < group:
        shifted = pltpu.roll(x, s, axis=1)
        x = x + jnp.where(pos >= s, shifted, 0.0)
        s *= 2
    return x


_B_NEG16, _B_C1, _B_C2, _B_EYE, _B_CAUSAL, _N_BMASKS = 0, 1, 2, 3, 4, 5


def _col_replicated(row):
    return jnp.transpose(jnp.broadcast_to(row, (LANES, SB)))


def _gdn_prep(grp, q_ref, k_ref, v_ref, st_ref, alog_ref, dtb_ref, cq_ref, ck_ref, cv_ref,
              eq_ref, ek_ref, ev_ref, dst):
    lq_ref, kk_ref, rhs_ref, qe_ref, kdt_ref, gcc_ref, grs_ref = dst
    qa = _conv_silu(eq_ref, q_ref, cq_ref)
    ka = _conv_silu(ek_ref, k_ref, ck_ref)
    va = _conv_silu(ev_ref, v_ref, cv_ref)
    for j in range(HG):
        hh = grp * HG + j
        beta_row = jax.nn.sigmoid(st_ref[pl.ds(hh, 1), :])
        xg = st_ref[pl.ds(A_HEADS + hh, 1), :] + dtb_ref[pl.ds(hh, 1), :]
        softplus = jnp.maximum(xg, 0.0) + jnp.log1p(jnp.exp(-jnp.abs(xg)))
        g_row = -jnp.exp(alog_ref[pl.ds(hh, 1), :]) * softplus
        gc_row8 = _group_cumsum_rows(jnp.broadcast_to(g_row, (8, SB)), CHUNK)
        gc = _col_replicated(gc_row8[0:1])
        bc = _col_replicated(beta_row)
        grs_ref[j] = gc_row8
        gcc_ref[j] = gc

        q = _l2norm(qa[:, j * A_DK:(j + 1) * A_DK]) * (A_DK ** -0.5)
        k = _l2norm(ka[:, j * A_DK:(j + 1) * A_DK])
        v = va[:, j * A_DV:(j + 1) * A_DV]
        eg = jnp.exp(gc)
        kb = k * bc
        glast_rep = jnp.concatenate(
            [jnp.broadcast_to(gc[c * CHUNK + CHUNK - 1:(c + 1) * CHUNK, :], (CHUNK, LANES))
             for c in range(CPS)], axis=0)
        for b in range(2):
            lq_ref[j, b] = jnp.concatenate([kb[b * HB:(b + 1) * HB], q[b * HB:(b + 1) * HB]],
                                           axis=0).astype(BF16)
        kk_ref[j] = k.astype(BF16)
        rhs_ref[j] = jnp.concatenate([v * jnp.concatenate([bc, bc], axis=1), kb * eg],
                                     axis=1).astype(BF16)
        qe_ref[j] = (q * eg).astype(BF16)
        kdt_ref[j] = jnp.transpose(k * jnp.exp(glast_rep - gc)).astype(BF16)


def _gdn_main(fresh, src, z_ref, na_ref, o_ref, state_ref, bmask_ref, sel_ref):
    lq_ref, kk_ref, rhs_ref, qe_ref, kdt_ref, gcc_ref, grs_ref = src
    heads = range(HG)

    kq = [[lax.dot_general(lq_ref[j, b], kk_ref[j, b * HB:(b + 1) * HB, :], _NT,
                           preferred_element_type=F32) for b in range(2)] for j in heads]

    def bdot(a, b):
        return [_dot(a[h], b[h]).astype(BF16) for h in range(2)]

    def masked_scores(j):
        gc, gr = gcc_ref[j], grs_ref[j][0:1]
        out = []
        for b in range(2):
            rs = slice(b * HB, (b + 1) * HB)
            decay = jnp.exp(jnp.minimum(gc[rs] - gr[:, rs], 0.0))
            ab = (kq[j][b][:HB] * decay).astype(BF16)
            n1 = ab * bmask_ref[_B_NEG16]
            qk = kq[j][b][HB:]
            out.append(((qk * decay).astype(BF16) * bmask_ref[_B_CAUSAL], n1,
                        bmask_ref[_B_EYE] + n1, ab * bmask_ref[_B_C1], ab * bmask_ref[_B_C2]))
        return [list(t) for t in zip(*out)]

    attn, n1b, pb, c1b, c2b = zip(*[masked_scores(j) for j in heads])
    eye = bmask_ref[_B_EYE]

    n2b = [bdot(n1b[j], n1b[j]) for j in heads]
    pb = [bdot(pb[j], [eye + m for m in n2b[j]]) for j in heads]
    n4b = [bdot(n2b[j], n2b[j]) for j in heads]
    pb = [bdot(pb[j], [eye + m for m in n4b[j]]) for j in heads]
    n8b = [bdot(n4b[j], n4b[j]) for j in heads]
    pb = [bdot(pb[j], [eye + m for m in n8b[j]]) for j in heads]
    tinv = pb
    for blk, cb in ((16, c1b), (32, c2b)):
        low = lambda x: jnp.concatenate(
            [x[b + blk:b + 2 * blk] for b in range(0, HB, 2 * blk)], axis=0)
        eye_low = low(eye)
        xm = [[eye_low - m for m in bdot([low(t) for t in tinv[j]], cb[j])] for j in heads]
        t_low = [bdot(xm[j], tinv[j]) for j in heads]
        tinv = [[jnp.concatenate(
            [piece for i, b in enumerate(range(0, HB, 2 * blk))
             for piece in (tinv[j][h][b:b + blk], t_low[j][h][i * blk:(i + 1) * blk])], axis=0)
            for h in range(2)] for j in heads]
    halves = range(2)
    uwb = [[_dot(tinv[j][h], rhs_ref[j, h * HB:(h + 1) * HB, :]).astype(BF16) for h in halves]
           for j in heads]
    aw = [[_dot(attn[j][h], uwb[j][h]) for h in halves] for j in heads]
    qeff = [jnp.concatenate(
        [(qe_ref[j, h * HB:(h + 1) * HB, :].astype(F32) - aw[j][h][:, A_DV:]).astype(BF16)
         for h in halves], axis=0) for j in heads]
    bp = [[_dot(jnp.concatenate([kdt_ref[j, :, h * HB:(h + 1) * HB]] * (CPS // 2), axis=0)
                * sel_ref[...], uwb[j][h]) for h in halves] for j in heads]

    st = [jnp.where(fresh, 0.0, state_ref[j]) for j in heads]
    outs = [[] for _ in heads]
    for c in range(CPS):
        for j in heads:
            blk = bp[j][c // 2][(c % 2) * A_DK:(c % 2 + 1) * A_DK]
            lhs = jnp.concatenate([blk[:, A_DV:].astype(BF16), qeff[j][c * CHUNK:(c + 1) * CHUNK]],
                                  axis=0)
            res = _dot(lhs, st[j].astype(BF16))
            outs[j].append(res[A_DK:])
            ld = jnp.exp(gcc_ref[j, c * CHUNK + CHUNK - 1:(c + 1) * CHUNK, :])
            st[j] = st[j] * jnp.concatenate([ld, ld], axis=1) + blk[:, :A_DV] - res[:A_DK]

    na = na_ref[...]
    for j in heads:
        state_ref[j] = st[j]
        o = jnp.concatenate(outs[j], axis=0) + jnp.concatenate([a[:, :A_DV] for a in aw[j]], axis=0)
        ms = jnp.mean(o * o, axis=-1, keepdims=True)
        gate = z_ref[:, j * A_DV:(j + 1) * A_DV].astype(F32)
        o_ref[:, j * A_DV:(j + 1) * A_DV] = (
            o * lax.rsqrt(ms + EPS) * na * gate).astype(o_ref.dtype)


def _gdn_kernel(q_ref, k_ref, v_ref, z_ref, st_ref, alog_ref, dtb_ref, cq_ref, ck_ref, cv_ref,
                na_ref, o_ref, eq_ref, ek_ref, ev_ref, state_ref, bmask_ref, sel_ref, *stage):
    grp = pl.program_id(1)
    r = pl.program_id(2)
    stage_a, stage_b = stage[:len(stage) // 2], stage[len(stage) // 2:]

    @pl.when((pl.program_id(0) == 0) & (grp == 0) & (r == 0))
    def _():
        row = lax.broadcasted_iota(jnp.int32, (HB, HB), 0)
        col = lax.broadcasted_iota(jnp.int32, (HB, HB), 1)

        def same(bits):
            return ((row >> bits) == (col >> bits)).astype(F32)

        m16, m32, m64 = same(4), same(5), same(6)
        lower = (row >= col).astype(F32)
        bmask_ref[_B_CAUSAL] = (m64 * lower).astype(BF16)
        bmask_ref[_B_NEG16] = (-(m16 * (row > col).astype(F32))).astype(BF16)
        bmask_ref[_B_C1] = ((m32 - m16) * lower).astype(BF16)
        bmask_ref[_B_C2] = ((m64 - m32) * lower).astype(BF16)
        bmask_ref[_B_EYE] = (row == col).astype(F32).astype(BF16)
        rsel = lax.broadcasted_iota(jnp.int32, (CPS // 2 * A_DK, HB), 0)
        csel = lax.broadcasted_iota(jnp.int32, (CPS // 2 * A_DK, HB), 1)
        sel_ref[...] = ((rsel >> 7) == (csel >> 6)).astype(F32).astype(BF16)
        for ref in stage_b:
            ref[...] = jnp.zeros_like(ref)
        state_ref[...] = jnp.zeros_like(state_ref)

    @pl.when(r == 0)
    def _():
        eq_ref[...] = jnp.zeros_like(eq_ref)
        ek_ref[...] = jnp.zeros_like(ek_ref)
        ev_ref[...] = jnp.zeros_like(ev_ref)

    def step(dst, src):
        _gdn_prep(grp, q_ref, k_ref, v_ref, st_ref, alog_ref, dtb_ref, cq_ref, ck_ref, cv_ref,
                  eq_ref, ek_ref, ev_ref, dst)
        _gdn_main(r <= 1, src, z_ref, na_ref, o_ref, state_ref, bmask_ref, sel_ref)

    @pl.when(r % 2 == 0)
    def _():
        step(stage_a, stage_b)

    @pl.when(r % 2 == 1)
    def _():
        step(stage_b, stage_a)


def _gdn(proj, small_t, alog_b, dtb_b, conv_a, norm_a, batch, seq):
    t = proj.shape[0]
    nr = seq // SB
    wq, wv = HG * A_DK, HG * A_DV
    k_off = A_QK // wq
    v_off = (2 * A_QK) // wv
    z_off = (2 * A_QK + A_V) // wv

    def staged(b, g, r):
        return b * nr + jnp.minimum(r, nr - 1)

    def solved(b, g, r):
        return b * nr + jnp.maximum(r - 1, 0)

    stage = [
        pltpu.VMEM((HG, 2, 2 * HB, A_DK), BF16),
        pltpu.VMEM((HG, SB, A_DK), BF16),
        pltpu.VMEM((HG, SB, A_DV + A_DK), BF16),
        pltpu.VMEM((HG, SB, A_DK), BF16),
        pltpu.VMEM((HG, A_DK, SB), BF16),
        pltpu.VMEM((HG, SB, LANES), F32),
        pltpu.VMEM((HG, 8, SB), F32),
    ]
    return pl.pallas_call(
        _gdn_kernel,
        grid=(batch, A_HEADS // HG, nr + 1),
        in_specs=[
            pl.BlockSpec((SB, wq), lambda b, g, r: (staged(b, g, r), g)),
            pl.BlockSpec((SB, wq), lambda b, g, r: (staged(b, g, r), k_off + g)),
            pl.BlockSpec((SB, wv), lambda b, g, r: (staged(b, g, r), v_off + g)),
            pl.BlockSpec((SB, wv), lambda b, g, r: (solved(b, g, r), z_off + g)),
            pl.BlockSpec((None, 2 * A_HEADS, SB), lambda b, g, r: (b, 0, jnp.minimum(r, nr - 1))),
            pl.BlockSpec((A_HEADS, SB), lambda b, g, r: (0, 0)),
            pl.BlockSpec((A_HEADS, SB), lambda b, g, r: (0, 0)),
            pl.BlockSpec((CONV_W, wq), lambda b, g, r: (0, g)),
            pl.BlockSpec((CONV_W, wq), lambda b, g, r: (0, k_off + g)),
            pl.BlockSpec((CONV_W, wv), lambda b, g, r: (0, v_off + g)),
            pl.BlockSpec((1, A_DV), lambda b, g, r: (0, 0)),
        ],
        out_specs=pl.BlockSpec((SB, wv), lambda b, g, r: (solved(b, g, r), g)),
        out_shape=jax.ShapeDtypeStruct((t, A_V), BF16),
        scratch_shapes=[
            pltpu.VMEM((8, wq), F32),
            pltpu.VMEM((8, wq), F32),
            pltpu.VMEM((8, wv), F32),
            pltpu.VMEM((HG, A_DK, A_DV), F32),
            pltpu.VMEM((_N_BMASKS, HB, HB), BF16),
            pltpu.VMEM((CPS // 2 * A_DK, HB), BF16),
        ] + stage + stage,
        compiler_params=pltpu.CompilerParams(
            dimension_semantics=("arbitrary", "arbitrary", "arbitrary"),
            vmem_limit_bytes=VMEM_LIMIT),
        name="gdn",
    )(proj, proj, proj, proj, small_t, alog_b, dtb_b, conv_a, conv_a, conv_a, norm_a)


_D_MASK, _D_Q, _D_K, _N_DEC = 0, 1, 2, 3


def _ret_kernel(q_ref, k_ref, v_ref, g_ref, nb_ref, o_ref, state_ref, dec_ref):
    r = pl.program_id(1)
    lb = q_ref.shape[0]
    heads = range(B_HEADS)

    @pl.when((pl.program_id(0) == 0) & (r == 0))
    def _():
        row = lax.broadcasted_iota(jnp.int32, (SB, SB), 0)
        col = lax.broadcasted_iota(jnp.int32, (SB, SB), 1)
        allow = ((col >> 6) <= (row >> 6)).astype(F32)
        dist = jnp.abs(row - col).astype(F32)
        rowf = row.astype(F32)
        for h in heads:
            lg = math.log1p(-(2.0 ** (-5.0 - h)))
            dec_ref[h, _D_MASK] = jnp.exp(lg * dist) * allow * (B_DK ** -0.5)
            dec_ref[h, _D_Q] = jnp.exp(lg * (rowf + 1.0))
            dec_ref[h, _D_K] = jnp.exp(lg * (SB - 1.0 - rowf)) * (B_DK ** -0.5)

    @pl.when(r == 0)
    def _():
        state_ref[...] = jnp.zeros_like(state_ref)

    nb = nb_ref[...]
    state = [state_ref[h] for h in heads]
    for s in range(lb // SB):
        rows = pl.ds(s * SB, SB)
        qb = [q_ref[rows, h * B_DK:(h + 1) * B_DK] for h in heads]
        kb = [k_ref[rows, h * B_DK:(h + 1) * B_DK] for h in heads]
        vb = [v_ref[rows, h * B_DV:(h + 1) * B_DV] for h in heads]
        sc = [lax.dot_general(qb[h], kb[h], _NT, preferred_element_type=F32) for h in heads]
        p = [(sc[h] * dec_ref[h, _D_MASK]).astype(BF16) for h in heads]
        qd = [(qb[h].astype(F32) * dec_ref[h, _D_Q]).astype(BF16) for h in heads]
        kd = [(kb[h].astype(F32) * dec_ref[h, _D_K]).astype(BF16) for h in heads]
        o = [_dot(p[h], vb[h]) + _dot(qd[h], state[h].astype(BF16)) for h in heads]
        upd = [lax.dot_general(kd[h], vb[h], _TN, preferred_element_type=F32) for h in heads]
        state = [state[h] * math.exp(SB * math.log1p(-(2.0 ** (-5.0 - h)))) + upd[h] for h in heads]
        for h in heads:
            cols = slice(h * B_DV, (h + 1) * B_DV)
            ms = jnp.mean(o[h] * o[h], axis=-1, keepdims=True)
            gate = g_ref[rows, cols].astype(F32)
            o_ref[rows, cols] = (o[h] * lax.rsqrt(ms + EPS) * nb[:, cols] * gate).astype(o_ref.dtype)
    for h in heads:
        state_ref[h] = state[h]


def _ret(proj, norm_b, batch, seq, lb):
    t = proj.shape[0]
    nr = seq // lb
    base = 2 * A_QK + 2 * A_V
    q_off = base // B_QK
    k_off = (base + B_QK) // B_QK
    v_off = (base + 2 * B_QK) // B_V
    g_off = (base + 2 * B_QK + B_V) // B_V

    return pl.pallas_call(
        _ret_kernel,
        grid=(batch, nr),
        in_specs=[
            pl.BlockSpec((lb, B_QK), lambda b, r: (b * nr + r, q_off)),
            pl.BlockSpec((lb, B_QK), lambda b, r: (b * nr + r, k_off)),
            pl.BlockSpec((lb, B_V), lambda b, r: (b * nr + r, v_off)),
            pl.BlockSpec((lb, B_V), lambda b, r: (b * nr + r, g_off)),
            pl.BlockSpec((1, B_V), lambda b, r: (0, 0)),
        ],
        out_specs=pl.BlockSpec((lb, B_V), lambda b, r: (b * nr + r, 0)),
        out_shape=jax.ShapeDtypeStruct((t, B_V), BF16),
        scratch_shapes=[pltpu.VMEM((B_HEADS, B_DK, B_DV), F32),
                        pltpu.VMEM((B_HEADS, _N_DEC, SB, SB), F32)],
        compiler_params=pltpu.CompilerParams(
            dimension_semantics=("arbitrary", "arbitrary"), vmem_limit_bytes=VMEM_LIMIT),
        name="ret",
    )(proj, proj, proj, proj, norm_b)


def _merge_kernel(oa_ref, ob_ref, ga_ref, gb_ref, x_ref, wa_ref, wb_ref, wo_ref, n_ref, o_ref):
    ya = _dot(oa_ref[...], wa_ref[...])
    yb = _dot(ob_ref[...], wb_ref[...])
    ga = jax.nn.sigmoid(ga_ref[...].astype(F32))
    gb = jax.nn.sigmoid(gb_ref[...].astype(F32))
    mix = _dot((ga * ya + gb * yb).astype(BF16), wo_ref[...])
    ms = jnp.mean(mix * mix, axis=-1, keepdims=True)
    o_ref[...] = x_ref[...] + mix * lax.rsqrt(ms + EPS) * n_ref[...]


def _merge(oa, ob, proj, x2, wa, wb, wo, gain, tm):
    t, d = x2.shape
    gate_off = (2 * A_QK + 2 * A_V + 2 * B_QK + 2 * B_V) // d
    const = lambda i: (0, 0)
    return pl.pallas_call(
        _merge_kernel,
        grid=(t // tm,),
        in_specs=[
            pl.BlockSpec((tm, A_V), lambda i: (i, 0)),
            pl.BlockSpec((tm, B_V), lambda i: (i, 0)),
            pl.BlockSpec((tm, d), lambda i: (i, gate_off)),
            pl.BlockSpec((tm, d), lambda i: (i, gate_off + 1)),
            pl.BlockSpec((tm, d), lambda i: (i, 0)),
            pl.BlockSpec((A_V, d), const),
            pl.BlockSpec((B_V, d), const),
            pl.BlockSpec((d, d), const),
            pl.BlockSpec((1, d), const),
        ],
        out_specs=pl.BlockSpec((tm, d), lambda i: (i, 0)),
        out_shape=jax.ShapeDtypeStruct((t, d), F32),
        compiler_params=pltpu.CompilerParams(
            dimension_semantics=("arbitrary",), vmem_limit_bytes=VMEM_LIMIT),
        name="merge",
    )(oa, ob, proj, proj, x2, wa, wb, wo, gain)


def _mlp_kernel(x_ref, gpre_ref, wu_ref, wd_ref, gpost_ref, o_ref, *, ff_tile):
    x = x_ref[...]
    ms = jnp.mean(x * x, axis=-1, keepdims=True)
    hn = (x * lax.rsqrt(ms + EPS) * gpre_ref[...]).astype(BF16)
    d_ff = wu_ref.shape[1]
    ff = jnp.zeros(x.shape, F32)
    for j in range(d_ff // ff_tile):
        up = _dot(hn, wu_ref[:, j * ff_tile:(j + 1) * ff_tile])
        act = jnp.square(jnp.maximum(up, 0.0)).astype(BF16)
        ff = ff + _dot(act, wd_ref[j * ff_tile:(j + 1) * ff_tile, :])
    ms2 = jnp.mean(ff * ff, axis=-1, keepdims=True)
    o_ref[...] = x + ff * lax.rsqrt(ms2 + EPS) * gpost_ref[...]


def _mlp(x1, gpre, wu, wd, gpost, tm, ff_tile):
    t, d = x1.shape
    d_ff = wu.shape[1]
    const = lambda i: (0, 0)
    return pl.pallas_call(
        functools.partial(_mlp_kernel, ff_tile=ff_tile),
        grid=(t // tm,),
        in_specs=[
            pl.BlockSpec((tm, d), lambda i: (i, 0)),
            pl.BlockSpec((1, d), const),
            pl.BlockSpec((d, d_ff), const),
            pl.BlockSpec((d_ff, d), const),
            pl.BlockSpec((1, d), const),
        ],
        out_specs=pl.BlockSpec((tm, d), lambda i: (i, 0)),
        out_shape=jax.ShapeDtypeStruct((t, d), F32),
        compiler_params=pltpu.CompilerParams(
            dimension_semantics=("arbitrary",), vmem_limit_bytes=VMEM_LIMIT),
        name="mlp",
    )(x1, gpre, wu, wd, gpost)


def _pick(n, candidates):
    for c in candidates:
        if n % c == 0:
            return c
    raise ValueError(f"no tile for {n} among {candidates}")


def _layer(x, n_mix_pre, n_mix_post, n_mlp_pre, n_mlp_post, w_in, conv_a, a_log, dt_bias,
           norm_a, norm_b, w_br_a, w_br_b, w_out, w_up, w_down):
    batch, seq, d = x.shape
    t = batch * seq
    assert seq % SB == 0
    x2 = x.reshape(t, d)

    n_small_lo = 2 * A_QK + 2 * A_V
    n_small_hi = n_small_lo + 2 * A_HEADS
    w16 = w_in.astype(BF16)
    w_right = w16[:, n_small_hi:]
    w_small = jnp.pad(w16[:, n_small_lo:n_small_hi], ((0, 0), (0, LANES - 2 * A_HEADS)))

    tm = _pick(seq, (1024, 512, 256))
    inv_freq = ROPE_BASE ** (-jnp.arange(0, B_DK, 2, dtype=F32) / B_DK)
    ang_lo = jnp.arange(tm, dtype=F32)[:, None] * inv_freq[None, :]
    ang_hi = (jnp.arange(seq // tm, dtype=F32) * tm)[:, None] * inv_freq[None, :]
    tables = (jnp.cos(ang_lo), jnp.sin(ang_lo), jnp.cos(ang_hi), jnp.sin(ang_hi))

    proj, small = _inproj(x2, n_mix_pre.reshape(1, d), w16, w_right, w_small, tables, tm, seq)

    small_t = jnp.transpose(small[:, :2 * A_HEADS].reshape(batch, seq, 2 * A_HEADS), (0, 2, 1))
    alog_b = jnp.broadcast_to(a_log.reshape(A_HEADS, 1), (A_HEADS, SB))
    dtb_b = jnp.broadcast_to(dt_bias.reshape(A_HEADS, 1), (A_HEADS, SB))
    oa = _gdn(proj, small_t, alog_b, dtb_b, conv_a, norm_a.reshape(1, A_DV), batch, seq)

    lb = _pick(seq, (512, 256))
    ob = _ret(proj, norm_b.reshape(1, B_V), batch, seq, lb)

    tm2 = _pick(t, (512, 256))
    x1 = _merge(oa, ob, proj, x2, w_br_a.astype(BF16), w_br_b.astype(BF16), w_out.astype(BF16),
                n_mix_post.reshape(1, d), tm2)
    out = _mlp(x1, n_mlp_pre.reshape(1, d), w_up.astype(BF16), w_down.astype(BF16),
               n_mlp_post.reshape(1, d), tm2, 1024)
    return out.reshape(batch, seq, d)


def kernel(x, norm_mix_pre, norm_mix_post, norm_mlp_pre, norm_mlp_post, w_in, conv_a, a_log,
           dt_bias, norm_a, norm_b, w_br_a, w_br_b, w_out, w_up, w_down):
    for l in range(w_in.shape[0]):
        x = _layer(x, norm_mix_pre[l], norm_mix_post[l], norm_mlp_pre[l], norm_mlp_post[l],
                   w_in[l], conv_a[l], a_log[l], dt_bias[l], norm_a[l], norm_b[l],
                   w_br_a[l], w_br_b[l], w_out[l], w_up[l], w_down[l])
    return x
```
